```python
import math
import jax, jax.numpy as jnp
from jax import lax
import numpy as np

D_MODEL = 1024
BATCH = 2
SEQ = 8192
DEPTH = 2

GRID_W = 64
CTX_LEN = 256
EPS = 1e-6
ROPE_BASE = 10000.0
HEAD_DIM = 64

SWA_HEADS = 6
SWA_KV_HEADS = 2
SWA_WINDOW = 128
SWA_BLOCK = 128
GDN_HEADS = 6
GDN_DK = 64
GDN_DV = 64
GDN_CONV = 5
GDN_CHUNK = 64
MLA_HEADS = 4
MLA_Q_RANK = 192
MLA_KV_RANK = 128
MLA_NOPE = 64
MLA_ROPE = 32
MLA_V = 64
MLA_BLOCK = 128

D_MIX = SWA_HEADS * HEAD_DIM + GDN_HEADS * GDN_DV + MLA_HEADS * MLA_V
GDN_QKV = 2 * GDN_HEADS * GDN_DK + GDN_HEADS * GDN_DV
PROJ_SPLITS = (SWA_HEADS * HEAD_DIM, SWA_KV_HEADS * HEAD_DIM, SWA_KV_HEADS * HEAD_DIM,
               GDN_QKV, GDN_HEADS * GDN_DV, 4 * GDN_HEADS,
               MLA_Q_RANK, MLA_KV_RANK, MLA_ROPE)
D_PROJ = sum(PROJ_SPLITS)

N_EXPERTS = 64
MOE_TOP_K = 8
MOE_GROUPS = 8
MOE_TOPK_GROUPS = 4
D_EXPERT = 256
D_SHARED = 256
ROUTED_SCALE = 2.5
MOE_BLOCK = 128

kernel_name = "hybrid_dit_hymba_swa_gdn_mla_moe"


def rms_norm(x, gain=None):
    xf = x.astype(jnp.float32)
    y = xf * lax.rsqrt(jnp.mean(xf * xf, axis=-1, keepdims=True) + EPS)
    if gain is not None:
        y = y * gain.astype(jnp.float32)
    return y.astype(x.dtype)


def l2_normalize(t):
    tf = t.astype(jnp.float32)
    return (tf * lax.rsqrt(jnp.sum(tf * tf, axis=-1, keepdims=True) + EPS)).astype(t.dtype)


def grid_positions(n_tokens):
    rows = n_tokens // GRID_W
    row = jnp.repeat(jnp.arange(rows, dtype=jnp.int32), GRID_W)
    col = jnp.tile(jnp.arange(GRID_W, dtype=jnp.int32), rows)
    return row, col


def rope_2d(x, row, col):
    half = x.shape[-1] // 2

    def rot(xh, pos):
        n = xh.shape[-1] // 2
        inv = ROPE_BASE ** (-jnp.arange(n, dtype=jnp.float32) / n)
        ang = pos.astype(jnp.float32)[:, None] * inv[None, :]
        cos, sin = jnp.cos(ang)[:, None, :], jnp.sin(ang)[:, None, :]
        x1, x2 = xh[..., :n], xh[..., n:]
        return jnp.concatenate([x1 * cos - x2 * sin, x1 * sin + x2 * cos], axis=-1)

    return jnp.concatenate([rot(x[..., :half], row), rot(x[..., half:], col)], axis=-1).astype(x.dtype)


def split_proj(p):
    offsets = np.cumsum(PROJ_SPLITS)[:-1].tolist()
    return jnp.split(p, offsets, axis=-1)


def modulation(cvec, w_ada, b_ada):
    return jnp.split(jax.nn.silu(cvec) @ w_ada + b_ada, 6, axis=-1)


def swa_latent(q, k, v, kc, vc, sink):
    B, S, H, d = q.shape
    Bk = SWA_BLOCK
    nb = S // Bk
    G = H // SWA_KV_HEADS
    qb = q.reshape(B, nb, Bk, SWA_KV_HEADS, G, d)

    def neighbours(t):
        tp = jnp.pad(t, ((0, 0), (Bk, Bk), (0, 0), (0, 0))).reshape(B, nb + 2, Bk, SWA_KV_HEADS, d)
        return jnp.concatenate([tp[:, :-2], tp[:, 1:-1], tp[:, 2:]], axis=2)

    kb, vb = neighbours(k), neighbours(v)
    blk = jnp.arange(nb)[:, None, None]
    qpos = blk * Bk + jnp.arange(Bk)[None, :, None]
    kpos = (blk - 1) * Bk + jnp.arange(3 * Bk)[None, None, :]
    valid = (jnp.abs(qpos - kpos) <= SWA_WINDOW) & (kpos >= 0) & (kpos < S)
    scale = d ** -0.5
    s_loc = jnp.einsum('bnqkgd,bnpkd->bnkgqp', qb, kb).astype(jnp.float32) * scale
    s_loc = jnp.where(valid[None, :, None, None], s_loc, -jnp.inf)
    s_ctx = jnp.einsum('bnqkgd,bpkd->bnkgqp', qb, kc).astype(jnp.float32) * scale
    s_sink = jnp.broadcast_to(sink.reshape(SWA_KV_HEADS, G, 1, 1).astype(jnp.float32), s_loc.shape[:-1] + (1,))
    p = jax.nn.softmax(jnp.concatenate([s_loc, s_ctx, s_sink], axis=-1), axis=-1).astype(v.dtype)
    o = (jnp.einsum('bnkgqp,bnpkd->bnqkgd', p[..., :3 * Bk], vb)
         + jnp.einsum('bnkgqp,bpkd->bnqkgd', p[..., 3 * Bk:-1], vc))
    return o.reshape(B, S, H * d)


def swa_context(qc, kc, vc, sink):
    B, L, H, d = qc.shape
    G = H // SWA_KV_HEADS
    qg = qc.reshape(B, L, SWA_KV_HEADS, G, d)
    s = jnp.einsum('bqkgd,bpkd->bkgqp', qg, kc).astype(jnp.float32) * d ** -0.5
    s_sink = jnp.broadcast_to(sink.reshape(SWA_KV_HEADS, G, 1, 1).astype(jnp.float32), s.shape[:-1] + (1,))
    p = jax.nn.softmax(jnp.concatenate([s, s_sink], axis=-1), axis=-1).astype(vc.dtype)
    return jnp.einsum('bkgqp,bpkd->bqkgd', p[..., :-1], vc).reshape(B, L, H * d)


def short_conv(x, w):
    return lax.conv_general_dilated(x, w[:, None, :].astype(x.dtype), window_strides=(1,), padding='SAME',
                                    dimension_numbers=('NWC', 'WIO', 'NWC'), feature_group_count=x.shape[-1])


def gdn_features(qkv, ba, conv_w, a_log, dt_bias):
    B, T, _ = qkv.shape
    qkv = jax.nn.silu(short_conv(qkv, conv_w))
    q, k, v = jnp.split(qkv, [GDN_HEADS * GDN_DK, 2 * GDN_HEADS * GDN_DK], axis=-1)
    q = l2_normalize(q.reshape(B, T, GDN_HEADS, GDN_DK))
    k = l2_normalize(k.reshape(B, T, GDN_HEADS, GDN_DK))
    v = v.reshape(B, T, GDN_HEADS, GDN_DV)
    baf = ba.astype(jnp.float32)
    beta = jax.nn.sigmoid(baf[..., :2 * GDN_HEADS]).reshape(B, T, 2, GDN_HEADS)
    a = baf[..., 2 * GDN_HEADS:].reshape(B, T, 2, GDN_HEADS)
    g = -jnp.exp(a_log.astype(jnp.float32)) * jax.nn.softplus(a + dt_bias.astype(jnp.float32))
    return q, k, v, beta, g


def gated_delta_chunked(q, k, v, g, beta, state0, with_output):
    B, T, H, dk = q.shape
    dv = v.shape[-1]
    out_dtype = v.dtype
    C = GDN_CHUNK
    n = T // C
    f32 = jnp.float32

    def to_chunks(t):
        return jnp.moveaxis(t.astype(f32).reshape((B, n, C, H) + t.shape[3:]), 3, 1)

    q, k, v, g, beta = map(to_chunks, (q, k, v, g, beta))
    gc = jnp.cumsum(g, axis=-1)
    incl = jnp.tril(jnp.ones((C, C), dtype=bool))
    strict = jnp.tril(jnp.ones((C, C), dtype=bool), -1)
    decay = jnp.exp(jnp.where(incl, gc[..., :, None] - gc[..., None, :], -jnp.inf))
    k_beta = k * beta[..., None]
    lower = jnp.where(strict, jnp.einsum('bhnid,bhnjd->bhnij', k_beta, k) * decay, 0.0)
    rhs = jnp.concatenate([v * beta[..., None], k_beta * jnp.exp(gc)[..., None]], axis=-1)
    sol = lax.linalg.triangular_solve(jnp.eye(C, dtype=f32) + lower, rhs,
                                      left_side=True, lower=True, unit_diagonal=True)
    u, w = sol[..., :dv], sol[..., dv:]
    k_out = k * jnp.exp(gc[..., -1:] - gc)[..., None]
    g_tot = jnp.exp(gc[..., -1])
    chunk_major = lambda t: jnp.moveaxis(t, 2, 0)
    if with_output:
        qs = q * dk ** -0.5
        intra = jnp.where(incl, jnp.einsum('bhnid,bhnjd->bhnij', qs, k) * decay, 0.0)
        xs = tuple(map(chunk_major, (u, w, k_out, g_tot, qs * jnp.exp(gc)[..., None], intra)))
    else:
        xs = tuple(map(chunk_major, (u, w, k_out, g_tot)))

    def step(S, inp):
        u_i, w_i, k_i, gt_i = inp[:4]
        v_new = u_i - jnp.einsum('bhcd,bhde->bhce', w_i, S)
        S_next = S * gt_i[..., None, None] + jnp.einsum('bhcd,bhce->bhde', k_i, v_new)
        if not with_output:
            return S_next, None
        q_i, a_i = inp[4:]
        o_i = jnp.einsum('bhcd,bhde->bhce', q_i, S) + jnp.einsum('bhij,bhje->bhie', a_i, v_new)
        return S_next, o_i

    S_final, o = lax.scan(step, state0.astype(f32), xs)
    if not with_output:
        return None, S_final
    o = jnp.moveaxis(jnp.moveaxis(o, 0, 2), 1, 3).reshape(B, T, H, dv)
    return o.astype(out_dtype), S_final


def gdn_bidirectional(lat, ctx, ctx_out):
    qL, kL, vL, bL, gL = lat
    qC, kC, vC, bC, gC = ctx
    rev = lambda t: jnp.flip(t, axis=1)
    S0 = jnp.zeros((qL.shape[0], GDN_HEADS, GDN_DK, GDN_DV), jnp.float32)
    oCf, SCf = gated_delta_chunked(qC, kC, vC, gC[:, :, 0], bC[:, :, 0], S0, ctx_out)
    oLf, _ = gated_delta_chunked(qL, kL, vL, gL[:, :, 0], bL[:, :, 0], SCf, True)
    oCb, SCb = gated_delta_chunked(rev(qC), rev(kC), rev(vC), rev(gC[:, :, 1]), rev(bC[:, :, 1]), S0, ctx_out)
    oLb, _ = gated_delta_chunked(rev(qL), rev(kL), rev(vL), rev(gL[:, :, 1]), rev(bL[:, :, 1]), SCb, True)
    o_lat = oLf + rev(oLb)
    o_ctx = oCf + rev(oCb) if ctx_out else None
    return o_lat, o_ctx


def gdn_output(o, z, norm_w):
    B, T = o.shape[:2]
    z = z.reshape(B, T, GDN_HEADS, GDN_DV)
    return (rms_norm(o, norm_w) * jax.nn.silu(z)).reshape(B, T, GDN_HEADS * GDN_DV)


def mla_queries(cq, q_norm, w_uq, row, col):
    B, T = cq.shape[:2]
    q = (rms_norm(cq, q_norm) @ w_uq).reshape(B, T, MLA_HEADS, MLA_NOPE + MLA_ROPE)
    qn, qr = q[..., :MLA_NOPE], q[..., MLA_NOPE:]
    if row is not None:
        qr = rope_2d(qr, row, col)
    return qn, qr


def mla_keys(ckv, kr, kv_norm, w_ukv, row, col):
    B, T = ckv.shape[:2]
    kv = (rms_norm(ckv, kv_norm) @ w_ukv).reshape(B, T, MLA_HEADS, MLA_NOPE + MLA_V)
    kn, v = kv[..., :MLA_NOPE], kv[..., MLA_NOPE:]
    if row is not None:
        kr = rope_2d(kr[:, :, None, :], row, col)[:, :, 0]
    return kn, kr, v


def mla_attend(qn, qr, kn, kr, v):
    s = jnp.einsum('bqhd,bkhd->bhqk', qn, kn) + jnp.einsum('bqhr,bkr->bhqk', qr, kr)
    p = jax.nn.softmax(s.astype(jnp.float32) * (MLA_NOPE + MLA_ROPE) ** -0.5, axis=-1).astype(v.dtype)
    return jnp.einsum('bhqk,bkhd->bqhd', p, v)


def mla_latent(qn, qr, kn, kr, v):
    B, S = qn.shape[:2]
    nb = S // MLA_BLOCK

    def blocks(t):
        return jnp.moveaxis(t.reshape((B, nb, MLA_BLOCK) + t.shape[2:]), 1, 0)

    o = lax.map(lambda qs: mla_attend(qs[0], qs[1], kn, kr, v), (blocks(qn), blocks(qr)))
    return jnp.moveaxis(o, 0, 1).reshape(B, S, MLA_HEADS * MLA_V)


def token_mixer(h, hc, w_in, w_out, swa_sink, conv_w, a_log, dt_bias, gdn_norm_w,
                q_norm, w_uq, kv_norm, w_ukv, row, col, ctx_out):
    B, S, _ = h.shape
    L = hc.shape[1]
    pl = split_proj(h @ w_in)
    pc = split_proj(hc @ w_in)
    qa = rope_2d(pl[0].reshape(B, S, SWA_HEADS, HEAD_DIM), row, col)
    ka = rope_2d(pl[1].reshape(B, S, SWA_KV_HEADS, HEAD_DIM), row, col)
    va = pl[2].reshape(B, S, SWA_KV_HEADS, HEAD_DIM)
    kca = pc[1].reshape(B, L, SWA_KV_HEADS, HEAD_DIM)
    vca = pc[2].reshape(B, L, SWA_KV_HEADS, HEAD_DIM)
    oa = swa_latent(qa, ka, va, kca, vca, swa_sink)
    featL = gdn_features(pl[3], pl[5], conv_w, a_log, dt_bias)
    featC = gdn_features(pc[3], pc[5], conv_w, a_log, dt_bias)
    obL, obC = gdn_bidirectional(featL, featC, ctx_out)
    ob = gdn_output(obL, pl[4], gdn_norm_w)
    qnL, qrL = mla_queries(pl[6], q_norm, w_uq, row, col)
    knL, krL, vL = mla_keys(pl[7], pl[8], kv_norm, w_ukv, row, col)
    knC, krC, vC = mla_keys(pc[7], pc[8], kv_norm, w_ukv, None, None)
    oc = mla_latent(qnL, qrL, jnp.concatenate([knL, knC], axis=1), jnp.concatenate([krL, krC], axis=1),
                    jnp.concatenate([vL, vC], axis=1))
    y = jnp.concatenate([oa, ob, oc], axis=-1) @ w_out
    if not ctx_out:
        return y, None
    oca = swa_context(pc[0].reshape(B, L, SWA_HEADS, HEAD_DIM), kca, vca, swa_sink)
    obc = gdn_output(obC, pc[4], gdn_norm_w)
    qnC, qrC = mla_queries(pc[6], q_norm, w_uq, None, None)
    occ = mla_attend(qnC, qrC, knC, krC, vC).reshape(B, L, MLA_HEADS * MLA_V)
    yc = jnp.concatenate([oca, obc, occ], axis=-1) @ w_out
    return y, yc


def swiglu(t, wg, wu, wd):
    return (jax.nn.silu(t @ wg) * (t @ wu)) @ wd


def route(h, router_w, router_bias):
    N = h.shape[0]
    scores = jax.nn.sigmoid((h @ router_w).astype(jnp.float32))
    sel = scores + router_bias.astype(jnp.float32)
    grouped = sel.reshape(N, MOE_GROUPS, N_EXPERTS // MOE_GROUPS)
    group_score = lax.top_k(grouped, 2)[0].sum(-1)
    _, top_groups = lax.top_k(group_score, MOE_TOPK_GROUPS)
    group_mask = jnp.any(top_groups[:, :, None] == jnp.arange(MOE_GROUPS)[None, None, :], axis=1)
    masked = jnp.where(jnp.repeat(group_mask, N_EXPERTS // MOE_GROUPS, axis=1), sel, -jnp.inf)
    _, top_idx = lax.top_k(masked, MOE_TOP_K)
    w = jnp.take_along_axis(scores, top_idx, axis=1)
    return top_idx, w / jnp.sum(w, axis=-1, keepdims=True) * ROUTED_SCALE


def moe_ffn(h, router_w, router_bias, w_gate, w_up, w_down, s_gate, s_up, s_down):
    N, D = h.shape
    top_idx, top_w = route(h, router_w, router_bias)
    NK = N * MOE_TOP_K
    e = top_idx.reshape(-1)
    tok = jnp.arange(NK, dtype=jnp.int32) // MOE_TOP_K
    wts = top_w.reshape(-1).astype(h.dtype)
    order = jnp.argsort(e)
    e_s = e[order]
    counts = jnp.bincount(e, length=N_EXPERTS)
    starts = jnp.cumsum(counts) - counts
    pcounts = (counts + MOE_BLOCK - 1) // MOE_BLOCK * MOE_BLOCK
    pends = jnp.cumsum(pcounts)
    dest = (pends - pcounts)[e_s] + jnp.arange(NK, dtype=jnp.int32) - starts[e_s]
    n_blocks = -(-NK // MOE_BLOCK) + N_EXPERTS
    P = n_blocks * MOE_BLOCK
    buf_tok = jnp.full((P,), N, jnp.int32).at[dest].set(tok[order])
    buf_w = jnp.zeros((P,), h.dtype).at[dest].set(wts[order])
    blk_exp = jnp.minimum(jnp.searchsorted(pends, jnp.arange(n_blocks) * MOE_BLOCK, side='right'), N_EXPERTS - 1)
    xb = jnp.concatenate([h, jnp.zeros((1, D), h.dtype)], axis=0)[buf_tok].reshape(n_blocks, MOE_BLOCK, D)
    yb = lax.map(lambda a: swiglu(a[0], w_gate[a[1]], w_up[a[1]], w_down[a[1]]), (xb, blk_exp))
    yb = yb.reshape(P, D) * buf_w[:, None]
    routed = jnp.zeros((N + 1, D), yb.dtype).at[buf_tok].add(yb)[:N]
    return swiglu(h, s_gate, s_up, s_down) + routed


def setup_inputs(seed: int = 0) -> dict:
    key = jax.random.key(seed)
    ks = iter(jax.random.split(key, 40))
    nrm = lambda shape, scale: scale * jax.random.normal(next(ks), shape, jnp.float32)
    gain = lambda shape: 1.0 + nrm(shape, 0.02)
    dt = jnp.exp(jax.random.uniform(next(ks), (DEPTH, 2, GDN_HEADS), jnp.float32,
                                    minval=math.log(1e-3), maxval=math.log(1e-1)))
    return {
        "x": nrm((BATCH, SEQ, D_MODEL), 1.0),
        "c": nrm((BATCH, D_MODEL), 1.0),
        "ctx": nrm((BATCH, CTX_LEN, D_MODEL), 1.0),
        "c_ctx": nrm((D_MODEL,), 1.0),
        "ada_w": nrm((DEPTH, D_MODEL, 6 * D_MODEL), 0.5 * D_MODEL ** -0.5),
        "ada_b": nrm((DEPTH, 6 * D_MODEL), 0.02),
        "mix_norm_pre": gain((DEPTH, D_MODEL)),
        "mix_norm_post": gain((DEPTH, D_MODEL)),
        "ffn_norm_pre": gain((DEPTH, D_MODEL)),
        "ffn_norm_post": gain((DEPTH, D_MODEL)),
        "w_in": nrm((DEPTH, D_MODEL, D_PROJ), D_MODEL ** -0.5),
        "w_out": nrm((DEPTH, D_MIX, D_MODEL), D_MIX ** -0.5),
        "swa_sink": nrm((DEPTH, SWA_HEADS), 0.5),
        "gdn_conv_w": nrm((DEPTH, GDN_CONV, GDN_QKV), GDN_CONV ** -0.5),
        "gdn_a_log": jnp.log(jax.random.uniform(next(ks), (DEPTH, 2, GDN_HEADS), jnp.float32, minval=1.0, maxval=16.0)),
        "gdn_dt_bias": dt + jnp.log(-jnp.expm1(-dt)),
        "gdn_norm_w": gain((DEPTH, GDN_DV)),
        "mla_q_norm": gain((DEPTH, MLA_Q_RANK)),
        "mla_w_uq": nrm((DEPTH, MLA_Q_RANK, MLA_HEADS * (MLA_NOPE + MLA_ROPE)), MLA_Q_RANK ** -0.5),
        "mla_kv_norm": gain((DEPTH, MLA_KV_RANK)),
        "mla_w_ukv": nrm((DEPTH, MLA_KV_RANK, MLA_HEADS * (MLA_NOPE + MLA_V)), MLA_KV_RANK ** -0.5),
        "router_w": nrm((DEPTH, D_MODEL, N_EXPERTS), D_MODEL ** -0.5),
        "router_bias": nrm((DEPTH, N_EXPERTS), 0.01),
        "expert_w_gate": nrm((DEPTH, N_EXPERTS, D_MODEL, D_EXPERT), D_MODEL ** -0.5),
        "expert_w_up": nrm((DEPTH, N_EXPERTS, D_MODEL, D_EXPERT), D_MODEL ** -0.5),
        "expert_w_down": nrm((DEPTH, N_EXPERTS, D_EXPERT, D_MODEL), D_EXPERT ** -0.5),
        "shared_w_gate": nrm((DEPTH, D_MODEL, D_SHARED), D_MODEL ** -0.5),
        "shared_w_up": nrm((DEPTH, D_MODEL, D_SHARED), D_MODEL ** -0.5),
        "shared_w_down": nrm((DEPTH, D_SHARED, D_MODEL), D_SHARED ** -0.5),
    }


def reference(x, c, ctx, c_ctx, ada_w, ada_b, mix_norm_pre, mix_norm_post, ffn_norm_pre, ffn_norm_post,
              w_in, w_out, swa_sink, gdn_conv_w, gdn_a_log, gdn_dt_bias, gdn_norm_w,
              mla_q_norm, mla_w_uq, mla_kv_norm, mla_w_ukv, router_w, router_bias,
              expert_w_gate, expert_w_up, expert_w_down, shared_w_gate, shared_w_up, shared_w_down):
    B, S, D = x.shape
    L = ctx.shape[1]
    row, col = grid_positions(S)
    for layer in range(DEPTH):
        last = layer == DEPTH - 1
        sh1, sc1, g1, sh2, sc2, g2 = [t[:, None, :] for t in modulation(c, ada_w[layer], ada_b[layer])]
        csh1, csc1, cg1, csh2, csc2, cg2 = modulation(c_ctx, ada_w[layer], ada_b[layer])
        h = rms_norm(x, mix_norm_pre[layer]) * (1.0 + sc1) + sh1
        hc = rms_norm(ctx, mix_norm_pre[layer]) * (1.0 + csc1) + csh1
        y, yc = token_mixer(h, hc, w_in[layer], w_out[layer], swa_sink[layer], gdn_conv_w[layer],
                            gdn_a_log[layer], gdn_dt_bias[layer], gdn_norm_w[layer], mla_q_norm[layer],
                            mla_w_uq[layer], mla_kv_norm[layer], mla_w_ukv[layer], row, col, not last)
        x = x + g1 * rms_norm(y, mix_norm_post[layer])
        moe_args = (router_w[layer], router_bias[layer], expert_w_gate[layer], expert_w_up[layer],
                    expert_w_down[layer], shared_w_gate[layer], shared_w_up[layer], shared_w_down[layer])
        h2 = rms_norm(x, ffn_norm_pre[layer]) * (1.0 + sc2) + sh2
        if last:
            f = moe_ffn(h2.reshape(B * S, D), *moe_args).reshape(B, S, D)
        else:
            ctx = ctx + cg1 * rms_norm(yc, mix_norm_post[layer])
            h2c = rms_norm(ctx, ffn_norm_pre[layer]) * (1.0 + csc2) + csh2
            f_all = moe_ffn(jnp.concatenate([h2.reshape(B * S, D), h2c.reshape(B * L, D)], axis=0), *moe_args)
            f, fc = f_all[:B * S].reshape(B, S, D), f_all[B * S:].reshape(B, L, D)
            ctx = ctx + cg2 * rms_norm(fc, ffn_norm_post[layer])
        x = x + g2 * rms_norm(f, ffn_norm_post[layer])
    return x
```

```python
import functools
import math

import numpy as np
import jax
import jax.numpy as jnp
from jax import lax
from jax.experimental import pallas as pl
from jax.experimental.pallas import tpu as pltpu

F32 = jnp.float32
BF16 = jnp.bfloat16

DEPTH = 2
GRID_W = 64
EPS = 1e-6
ROPE_BASE = 10000.0
HEAD_DIM = 64
SWA_HEADS = 6
SWA_KV_HEADS = 2
SWA_WINDOW = 128
GDN_HEADS = 6
GDN_DK = 64
GDN_DV = 64
GDN_CONV = 5
GDN_CHUNK = 64
MLA_HEADS = 4
MLA_Q_RANK = 192
MLA_KV_RANK = 128
MLA_NOPE = 64
MLA_ROPE = 32
MLA_V = 64
N_EXPERTS = 64
MOE_TOP_K = 8
MOE_GROUPS = 8
MOE_TOPK_GROUPS = 4
D_EXPERT = 256
ROUTED_SCALE = 2.5

_SPLITS = (384, 128, 128, 1152, 384, 24, 192, 128, 32)
_OFF = np.concatenate([[0], np.cumsum(_SPLITS)]).tolist()
D_PROJ = _OFF[-1]

TM = 256
LANE = 128
VMEM_LIMIT = 56 * 1024 * 1024

_C_QA, _C_QAS, _C_KA, _C_KAS, _C_VA, _C_GDN, _C_Z, _C_CQ, _C_CKV, _C_KRB, _C_END = (
    0, 768, 1536, 1664, 1792, 1920, 3072, 3456, 3712, 3840, 3968)
_BA_LANE = 32


def _cparams(sem, vmem=VMEM_LIMIT):
    return pltpu.CompilerParams(dimension_semantics=sem, vmem_limit_bytes=vmem)


def _rope_partner(d, width):
    half, n = width // 2, width // 4
    i = d % half
    return (d // half) * half + (i + n if i < n else i - n)


def _take_cols(w, idx):
    idx = np.asarray(idx)
    wz = jnp.concatenate([w, jnp.zeros((w.shape[0], 1), w.dtype)], axis=1)
    return jnp.take(wz, jnp.asarray(np.where(idx < 0, w.shape[1], idx)), axis=1)


def _prep_w_in(w_in):
    cols = []
    for swap in (False, True):
        for h in range(SWA_HEADS):
            j = h // (SWA_HEADS // SWA_KV_HEADS)
            blk = [-1] * LANE
            for d in range(HEAD_DIM):
                blk[64 * j + d] = _OFF[0] + h * HEAD_DIM + (_rope_partner(d, HEAD_DIM) if swap else d)
            cols += blk
    for swap in (False, True):
        for j in range(SWA_KV_HEADS):
            cols += [_OFF[1] + j * HEAD_DIM + (_rope_partner(d, HEAD_DIM) if swap else d) for d in range(HEAD_DIM)]
    cols += list(range(_OFF[2], _OFF[3]))
    cols += list(range(_OFF[3], _OFF[4]))
    cols += list(range(_OFF[4], _OFF[5]))
    cols += list(range(_OFF[6], _OFF[7])) + [-1] * 64
    cols += list(range(_OFF[7], _OFF[8]))
    cols += list(range(_OFF[8], _OFF[9])) + list(range(_OFF[5], _OFF[6])) + [-1] * (LANE - 32 - 24)
    assert len(cols) == _C_END
    return _take_cols(w_in, cols).astype(BF16)


def _prep_w_uq(w_uq):
    cols = []
    for swap in (False, True):
        for h in range(MLA_HEADS):
            base = h * (MLA_NOPE + MLA_ROPE)
            blk = [-1] * LANE
            for d in range(MLA_NOPE):
                blk[d] = -1 if swap else base + d
            for r in range(MLA_ROPE):
                blk[MLA_NOPE + r] = base + MLA_NOPE + (_rope_partner(r, MLA_ROPE) if swap else r)
            cols += blk
    w = _take_cols(w_uq, cols)
    return jnp.pad(w, ((0, 256 - MLA_Q_RANK), (0, 0))).astype(BF16)


def _prep_w_ukv(w_ukv):
    kcols, vcols = [], []
    for h in range(MLA_HEADS):
        base = h * (MLA_NOPE + MLA_V)
        kcols += [base + d for d in range(MLA_NOPE)] + [-1] * 64
        vcols += [base + MLA_NOPE + d for d in range(MLA_V)]
    wk = _take_cols(w_ukv, kcols)
    wv = _take_cols(w_ukv, vcols)
    top = jnp.concatenate([wk, jnp.zeros_like(wk), wv], axis=1)
    place = np.zeros((128, 1280), np.float32)
    for h in range(MLA_HEADS):
        for r in range(MLA_ROPE):
            place[r, 128 * h + MLA_NOPE + r] = 1.0
            place[_rope_partner(r, MLA_ROPE), 512 + 128 * h + MLA_NOPE + r] = 1.0
    return jnp.concatenate([top, jnp.asarray(place)], axis=0).astype(BF16)


def _prep_w_out(w_out):
    rows = []
    G = SWA_HEADS // SWA_KV_HEADS
    for g in range(G):
        for j in range(SWA_KV_HEADS):
            rows += [(G * j + g) * HEAD_DIM + d for d in range(HEAD_DIM)]
    rows += list(range(SWA_HEADS * HEAD_DIM, w_out.shape[0]))
    return jnp.take(w_out, jnp.asarray(rows), axis=0).astype(BF16)


def _rope_tables(S, T):
    t = np.arange(S)
    row, col = t // GRID_W, t % GRID_W

    def tab(width, lanes):
        half, n = width // 2, width // 4
        c = np.ones((T, LANE), np.float64)
        s = np.zeros((T, LANE), np.float64)
        for lane, d in lanes:
            i = d % half
            pos = row if d < half else col
            ang =(pos.astype(np.float32) * np.float32(ROPE_BASE ** (-(i % n) / n))).astype(np.float64)
            c[:S, lane] = np.cos(ang)
            s[:S, lane] = -np.sin(ang) if i < n else np.sin(ang)
        return jnp.asarray(c, F32), jnp.asarray(s, F32)

    ca, sa = tab(HEAD_DIM, [(l, l % HEAD_DIM) for l in range(LANE)])
    cm, sm = tab(MLA_ROPE, [(MLA_NOPE + r, r) for r in range(MLA_ROPE)])
    return ca, sa, cm, sm


def _mod_kernel(c_ref, w_ref, b_ref, o_ref):
    cv = c_ref[...]
    a = (cv * jax.nn.sigmoid(cv)).astype(BF16)
    o_ref[0] = jnp.dot(a, w_ref[0].astype(BF16), preferred_element_type=F32) + b_ref[0]


def _modulation(cvecs, ada_w, ada_b):
    depth, D, N = ada_w.shape
    tn = 512
    return pl.pallas_call(
        _mod_kernel,
        grid=(depth, N // tn),
        in_specs=[pl.BlockSpec((8, D), lambda l, j: (0, 0)),
                  pl.BlockSpec((1, D, tn), lambda l, j: (l, 0, j)),
                  pl.BlockSpec((1, 1, tn), lambda l, j: (l, 0, j))],
        out_specs=pl.BlockSpec((1, 8, tn), lambda l, j: (l, 0, j)),
        out_shape=jax.ShapeDtypeStruct((depth, 8, N), F32),
        compiler_params=_cparams(("arbitrary", "arbitrary")),
        name="modulation",
    )(cvecs, ada_w, ada_b.reshape(depth, 1, N))


def _proj_kernel(x_ref, mod_ref, gain_ref, w_ref, wq2_ref, wk2_ref, qg_ref, kvg_ref,
                 ca_ref, sa_ref, cm_ref, sm_ref,
                 qa_ref, ka_ref, va_ref, gx_ref, z_ref, ba_ref, qm_ref, km_ref, vm_ref):
    x = x_ref[0]
    ms = jnp.mean(x * x, axis=-1, keepdims=True)
    h = x * lax.rsqrt(ms + EPS) * gain_ref[...]
    h = h * (1.0 + mod_ref[0, 1:2, :]) + mod_ref[0, 0:1, :]
    hb = h.astype(BF16)

    ca, sa, cm, sm = ca_ref[...], sa_ref[...], cm_ref[...], sm_ref[...]

    def rope(a, b, c, s):
        n = a.shape[1] // LANE
        return a * jnp.concatenate([c] * n, axis=1) + b * jnp.concatenate([s] * n, axis=1)

    p1 = jnp.dot(hb, w_ref[:, _C_QA:_C_GDN], preferred_element_type=F32)
    qa = rope(p1[:, _C_QA:_C_QAS], p1[:, _C_QAS:_C_KA], ca, sa)
    qa_ref[0] = (qa * (HEAD_DIM ** -0.5)).astype(BF16)
    ka_ref[0] = rope(p1[:, _C_KA:_C_KAS], p1[:, _C_KAS:_C_VA], ca, sa).astype(BF16)
    va_ref[0] = p1[:, _C_VA:_C_GDN].astype(BF16)

    p2 = jnp.dot(hb, w_ref[:, _C_GDN:_C_Z], preferred_element_type=F32)
    for j in range(3 * GDN_HEADS):
        gx_ref[0, j] = p2[:, 64 * j:64 * j + 64]

    p3 = jnp.dot(hb, w_ref[:, _C_Z:_C_END], preferred_element_type=F32)
    z_ref[0] = p3[:, 0:_C_CQ - _C_Z]
    cq = p3[:, _C_CQ - _C_Z:_C_CKV - _C_Z]
    ckv = p3[:, _C_CKV - _C_Z:_C_KRB - _C_Z]
    krb = p3[:, _C_KRB - _C_Z:]
    ba_ref[0] = krb

    cqn = cq * lax.rsqrt(jnp.sum(cq * cq, axis=-1, keepdims=True) * (1.0 / MLA_Q_RANK) + EPS) * qg_ref[...]
    e = jnp.dot(cqn.astype(BF16), wq2_ref[...], preferred_element_type=F32)
    qm = rope(e[:, :512], e[:, 512:], cm, sm)
    qm_ref[0] = (qm * ((MLA_NOPE + MLA_ROPE) ** -0.5)).astype(BF16)

    ckvn = ckv * lax.rsqrt(jnp.mean(ckv * ckv, axis=-1, keepdims=True) + EPS) * kvg_ref[...]
    lhs2 = jnp.concatenate([ckvn.astype(BF16), krb.astype(BF16)], axis=1)
    e2 = jnp.dot(lhs2, wk2_ref[...], preferred_element_type=F32)
    km_ref[0] = rope(e2[:, :512], e2[:, 512:1024], cm, sm).astype(BF16)
    vm_ref[0] = e2[:, 1024:].astype(BF16)


def _const_spec(shape):
    nd = len(shape)
    return pl.BlockSpec(shape, lambda *_: (0,) * nd)


def _project(xx, modl, gain, w_main, wq2, wk2, qg, kvg, tabs, n_lat_tiles):
    B, T, D = xx.shape
    nt = T // TM
    nb = modl.shape[0] - 1

    def mod_map(b, t):
        return (jnp.where(t >= n_lat_tiles, nb, b), 0, 0)

    row = lambda w: pl.BlockSpec((1, TM, w), lambda b, t: (b, t, 0))
    tab = pl.BlockSpec((TM, LANE), lambda b, t: (t, 0))
    out_shapes = [
        jax.ShapeDtypeStruct((B, T, 768), BF16),
        jax.ShapeDtypeStruct((B, T, 128), BF16),
        jax.ShapeDtypeStruct((B, T, 128), BF16),
        jax.ShapeDtypeStruct((B, 18, T, 64), F32),
        jax.ShapeDtypeStruct((B, T, 384), F32),
        jax.ShapeDtypeStruct((B, T, 128), F32),
        jax.ShapeDtypeStruct((B, T, 512), BF16),
        jax.ShapeDtypeStruct((B, T, 512), BF16),
        jax.ShapeDtypeStruct((B, T, 256), BF16),
    ]
    out_specs = [row(768), row(128), row(128),
                 pl.BlockSpec((1, 18, TM, 64), lambda b, t: (b, 0, t, 0)),
                 row(384), row(128), row(512), row(512), row(256)]
    return pl.pallas_call(
        _proj_kernel,
        grid=(B, nt),
        in_specs=[row(D), pl.BlockSpec((1, 6, D), mod_map), _const_spec((1, D)),
                  _const_spec(w_main.shape), _const_spec(wq2.shape), _const_spec(wk2.shape),
                  _const_spec((1, 256)), _const_spec((1, 128)), tab, tab, tab, tab],
        out_specs=out_specs,
        out_shape=out_shapes,
        compiler_params=_cparams(("parallel", "parallel")),
        name="in_proj",
    )(xx, modl, gain, w_main, wq2, wk2, qg, kvg, *tabs)


def _nt_dot(a, b):
    return lax.dot_general(a, b, (((1,), (1,)), ((), ())), preferred_element_type=F32)


def _swa_kernel(sink_ref, q_ref, k_ref, v_ref, o_ref, *, S, n_lat):
    i = pl.program_id(1)
    G = SWA_HEADS // SWA_KV_HEADS
    W = 2 * TM
    lane = lax.broadcasted_iota(jnp.int32, (TM, LANE), 1)
    kc = k_ref[0, pl.ds(S, TM), :]
    vc = v_ref[0, pl.ds(S, TM), :]

    def heads(local):
        outs = []
        for g in range(G):
            og = []
            for j in range(SWA_KV_HEADS):
                h = G * j + g
                q = q_ref[0, :, LANE * h:LANE * (h + 1)]
                sink = sink_ref[h]
                s_ctx = _nt_dot(q, kc)
                m = jnp.maximum(jnp.max(s_ctx, axis=-1, keepdims=True), sink)
                if local is not None:
                    kw, vw, valid = local
                    s_loc = jnp.where(valid, _nt_dot(q, kw), -jnp.inf)
                    m = jnp.maximum(m, jnp.max(s_loc, axis=-1, keepdims=True))
                p_ctx = jnp.exp(s_ctx - m)
                den = jnp.sum(p_ctx, axis=-1, keepdims=True) + jnp.exp(sink - m)
                o = jnp.dot(p_ctx.astype(BF16), vc, preferred_element_type=F32)
                if local is not None:
                    p_loc = jnp.exp(s_loc - m)
                    den = den + jnp.sum(p_loc, axis=-1, keepdims=True)
                    o = o + jnp.dot(p_loc.astype(BF16), vw, preferred_element_type=F32)
                og.append(o * (1.0 / den))
            outs.append(jnp.where(lane < HEAD_DIM, og[0], og[1]))
        o_ref[0] = jnp.concatenate(outs, axis=1).astype(BF16)

    @pl.when(i < n_lat)
    def _latent():
        start = pl.multiple_of(jnp.clip(i * TM - SWA_WINDOW, 0, S - W), LANE)
        kw = k_ref[0, pl.ds(start, W), :]
        vw = v_ref[0, pl.ds(start, W), :]
        qpos = i * TM + lax.broadcasted_iota(jnp.int32, (TM, W), 0)
        kpos = start + lax.broadcasted_iota(jnp.int32, (TM, W), 1)
        heads((kw, vw, jnp.abs(qpos - kpos) <= SWA_WINDOW))

    @pl.when(i >= n_lat)
    def _context():
        heads(None)


def _swa_attention(sink, qa, ka, va, S, nt):
    B, T, _ = qa.shape
    kern = functools.partial(_swa_kernel, S=S, n_lat=S // TM)
    return pl.pallas_call(
        kern,
        grid=(B, nt),
        in_specs=[pl.BlockSpec(memory_space=pltpu.SMEM),
                  pl.BlockSpec((1, TM, 768), lambda b, t: (b, t, 0)),
                  pl.BlockSpec((1, T, 128), lambda b, t: (b, 0, 0)),
                  pl.BlockSpec((1, T, 128), lambda b, t: (b, 0, 0))],
        out_specs=pl.BlockSpec((1, TM, 384), lambda b, t: (b, t, 0)),
        out_shape=jax.ShapeDtypeStruct((B, nt * TM, 384), BF16),
        compiler_params=_cparams(("parallel", "parallel")),
        name="swa_attention",
    )(sink, qa, ka, va)


MLA_KV_CHUNK = 512


def _mla_kernel(q_ref, k_ref, v_ref, o_ref, *, S, n_lat):
    i = pl.program_id(1)
    lane = lax.broadcasted_iota(jnp.int32, (TM, LANE), 1)
    n_chunks = jnp.where(i < n_lat, S // MLA_KV_CHUNK, 0)
    outs = []
    for h in range(MLA_HEADS):
        q = q_ref[0, :, LANE * h:LANE * (h + 1)]
        kl, vl = LANE * h, LANE * (h // 2)

        def step(carry, kk, vv):
            m, l, acc = carry
            s = _nt_dot(q, kk)
            m_new = jnp.maximum(m, jnp.max(s, axis=-1, keepdims=True))
            alpha = jnp.exp(m - m_new)
            p = jnp.exp(s - m_new)
            l = alpha * l + jnp.sum(p, axis=-1, keepdims=True)
            acc = alpha * acc + jnp.dot(p.astype(BF16), vv, preferred_element_type=F32)
            return m_new, l, acc

        def chunk(c, carry):
            off = pl.multiple_of(c * MLA_KV_CHUNK, MLA_KV_CHUNK)
            return step(carry, k_ref[0, pl.ds(off, MLA_KV_CHUNK), kl:kl + LANE],
                        v_ref[0, pl.ds(off, MLA_KV_CHUNK), vl:vl + LANE])

        init = (jnp.full((TM, 1), -jnp.inf, F32), jnp.zeros((TM, 1), F32), jnp.zeros((TM, LANE), F32))
        carry = lax.fori_loop(0, n_chunks, chunk, init)
        m, l, acc = step(carry, k_ref[0, pl.ds(S, TM), kl:kl + LANE], v_ref[0, pl.ds(S, TM), vl:vl + LANE])
        outs.append(acc * (1.0 / l))
    o_ref[0] = jnp.concatenate([jnp.where(lane < MLA_V, outs[0], outs[1]),
                                jnp.where(lane < MLA_V, outs[2], outs[3])], axis=1).astype(BF16)


def _mla_attention(qm, km, vm, S, nt):
    B, T, _ = qm.shape
    kern = functools.partial(_mla_kernel, S=S, n_lat=S // TM)
    return pl.pallas_call(
        kern,
        grid=(B, nt),
        in_specs=[pl.BlockSpec((1, TM, 512), lambda b, t: (b, t, 0)),
                  pl.BlockSpec((1, T, 512), lambda b, t: (b, 0, 0)),
                  pl.BlockSpec((1, T, 256), lambda b, t: (b, 0, 0))],
        out_specs=pl.BlockSpec((1, TM, 256), lambda b, t: (b, t, 0)),
        out_shape=jax.ShapeDtypeStruct((B, nt * TM, 256), BF16),
        compiler_params=_cparams(("parallel", "parallel")),
        name="mla_attention",
    )(qm, km, vm)


CH = GDN_CHUNK
NCH = TM // CH


def _split3(a):
    hi = a.astype(BF16)
    r = a - hi.astype(F32)
    mid = r.astype(BF16)
    return hi, mid, (r - mid.astype(F32)).astype(BF16)


def _bmm3(a, b):
    ah, al, _ = _split3(a)
    bh, bl, _ = _split3(b)
    f = lambda x, y: jnp.einsum('cij,cjk->cik', x, y, preferred_element_type=F32)
    return f(ah, bh) + (f(ah, bl) + f(al, bh))


def _gdn_local_kernel(x_ref, xp_ref, xn_ref, cw_ref, ba_ref, gp_ref,
                      p1_ref, p2_ref, p3_ref, gt_ref, xs_ref, *, n_lat):
    h = pl.program_id(1)
    t = pl.program_id(2)
    has_prev = jnp.logical_and(t > 0, t != n_lat)
    has_next = t < n_lat - 1
    parts = []
    for part in range(3):
        xs_ref[part, 0:8, :] = jnp.where(has_prev, xp_ref[0, part, 0], 0.0)
        xs_ref[part, 8:8 + TM, :] = x_ref[0, part, 0]
        xs_ref[part, 8 + TM:16 + TM, :] = jnp.where(has_next, xn_ref[0, part, 0], 0.0)
        acc = jnp.zeros((TM, GDN_DK), F32)
        for k in range(GDN_CONV):
            acc = acc + cw_ref[0, part, k:k + 1, :] * xs_ref[part, pl.ds(8 - GDN_CONV // 2 + k, TM), :]
        parts.append(acc * jax.nn.sigmoid(acc))
    q_, k_, v = parts
    q = q_ * lax.rsqrt(jnp.sum(q_ * q_, axis=-1, keepdims=True) + EPS) * (GDN_DK ** -0.5)
    k = k_ * lax.rsqrt(jnp.sum(k_ * k_, axis=-1, keepdims=True) + EPS)

    baf = ba_ref[0]
    lane = lax.broadcasted_iota(jnp.int32, (TM, LANE), 1)
    beta_all = jax.nn.sigmoid(baf)
    xg = baf + gp_ref[1:2, :]
    g_all = -jnp.exp(gp_ref[0:1, :]) * (jnp.maximum(xg, 0.0) + jnp.log(1.0 + jnp.exp(-jnp.abs(xg))))

    def col(a, idx):
        cvec = jnp.sum(jnp.where(lane == idx, a, 0.0), axis=1, keepdims=True)
        return jnp.broadcast_to(cvec, (TM, CH)).reshape(NCH, CH, CH)

    q3, k3, v3 = (a.reshape(NCH, CH, GDN_DK) for a in (q, k, v))
    kb, qb = k3.astype(BF16), q3.astype(BF16)
    nt = lambda a, b: jnp.einsum('cid,cjd->cij', a, b, preferred_element_type=F32)
    kk, qk = nt(kb, kb), nt(qb, kb)
    ii = lax.broadcasted_iota(jnp.int32, (CH, CH), 0)
    jj = lax.broadcasted_iota(jnp.int32, (CH, CH), 1)
    eye = (ii == jj).astype(F32)[None]

    for d in range(2):
        beta = col(beta_all, _BA_LANE + GDN_HEADS * d + h)
        g = col(g_all, _BA_LANE + 2 * GDN_HEADS + GDN_HEADS * d + h)
        incl = (jj <= ii) if d == 0 else (jj >= ii)
        strict = (jj < ii) if d == 0 else (jj > ii)
        tri = jnp.broadcast_to(incl.astype(BF16)[None], (NCH, CH, CH))
        gcx = sum(jnp.einsum('cij,cjl->cil', tri, gpart, preferred_element_type=F32) for gpart in _split3(g))
        gcr = jnp.stack([gcx[c].T for c in range(NCH)], axis=0)
        decay = jnp.exp(jnp.where(incl[None], gcx - gcr, -jnp.inf))
        lm = jnp.where(strict[None], beta * kk * decay, 0.0)
        egc = jnp.exp(gcx)
        rhs = jnp.concatenate([v3 * beta, k3 * beta * egc], axis=-1)
        p = -lm
        tinv = eye + p
        for _ in range(5):
            p = _bmm3(p, p)
            tinv = tinv + _bmm3(tinv, p)
        uw = _bmm3(tinv, rhs)
        end = CH - 1 if d == 0 else 0
        gce = gcx[:, end:end + 1, :]
        k_out = k3 * jnp.exp(gce - gcx)
        p1_ref[d, 0, 0] = uw.reshape(TM, 2 * CH).astype(BF16)
        p2_ref[d, 0, 0] = jnp.concatenate([k_out, q3 * egc], axis=-1).reshape(TM, 2 * CH).astype(BF16)
        p3_ref[d, 0, 0] = (qk * decay).reshape(TM, CH).astype(BF16)
        gte = jnp.exp(gce)
        gt_ref[d, 0, 0] = jnp.broadcast_to(jnp.concatenate([gte, gte], axis=-1), (NCH, 8, LANE)).reshape(NCH * 8, LANE)


def _gdn_local(gx, conv_w, ba, gp, n_lat):
    B, _, T, _ = gx.shape
    H = GDN_HEADS
    nt = T // TM
    gx5 = gx.reshape(B, 3, H, T, GDN_DK)
    cw = jnp.transpose(conv_w.reshape(GDN_CONV, 3, H, GDN_DK), (2, 1, 0, 3))
    kern = functools.partial(_gdn_local_kernel, n_lat=n_lat)
    r8 = TM // 8
    big = lambda w, dt: jax.ShapeDtypeStruct((2, B, H, T, w), dt)
    ospec = lambda w: pl.BlockSpec((2, 1, 1, TM, w), lambda b, h, t: (0, b, h, t, 0))
    return pl.pallas_call(
        kern,
        grid=(B, H, nt),
        in_specs=[pl.BlockSpec((1, 3, 1, TM, GDN_DK), lambda b, h, t: (b, 0, h, t, 0)),
                  pl.BlockSpec((1, 3, 1, 8, GDN_DK), lambda b, h, t: (b, 0, h, jnp.maximum(t * r8 - 1, 0), 0)),
                  pl.BlockSpec((1, 3, 1, 8, GDN_DK), lambda b, h, t: (b, 0, h, jnp.minimum((t + 1) * r8, T // 8 - 1), 0)),
                  pl.BlockSpec((1, 3, GDN_CONV, GDN_DK), lambda b, h, t: (h, 0, 0, 0)),
                  pl.BlockSpec((1, TM, LANE), lambda b, h, t: (b, t, 0)),
                  pl.BlockSpec((2, LANE), lambda b, h, t: (0, 0))],
        out_specs=[ospec(2 * CH), ospec(2 * CH), ospec(CH),
                   pl.BlockSpec((2, 1, 1, NCH * 8, LANE), lambda b, h, t: (0, b, h, t, 0))],
        out_shape=[big(2 * CH, BF16), big(2 * CH, BF16), big(CH, BF16),
                   jax.ShapeDtypeStruct((2, B, H, (T // CH) * 8, LANE), F32)],
        scratch_shapes=[pltpu.VMEM((3, TM + 16, GDN_DK), F32)],
        compiler_params=_cparams(("parallel", "parallel", "parallel")),
        name="gdn_local",
    )(gx5, gx5, gx5, cw, ba, gp)


def _gdn_scan_kernel(p1f, p2f, p3f, gtf, p1b, p2b, p3b, gtb, of_ref, ob_ref, s_ref):
    @pl.when(pl.program_id(0) == 0)
    def _init():
        s_ref[...] = jnp.zeros_like(s_ref)

    nb, nh = s_ref.shape[1], s_ref.shape[2]
    for d, (p1, p2, p3, gt, o_ref) in enumerate(((p1f, p2f, p3f, gtf, of_ref), (p1b, p2b, p3b, gtb, ob_ref))):
        for b in range(nb):
            for h in range(nh):
                s = s_ref[d, b, h]
                uw, kq = p1[0, b, h], p2[0, b, h]
                m1 = jnp.dot(jnp.concatenate([uw[:, CH:], kq[:, CH:]], axis=0), s.astype(BF16),
                             preferred_element_type=F32)
                v_new = (uw[:, :CH].astype(F32) - m1[:CH]).astype(BF16)
                o_ref[b, h] = m1[CH:] + jnp.dot(p3[0, b, h], v_new, preferred_element_type=F32)
                ds = lax.dot_general(kq[:, :CH], v_new, (((0,), (0,)), ((), ())), preferred_element_type=F32)
                s_ref[d, b, h] = s * gt[0, b, h][0:1, :CH] + ds


def _gdn_scan(p1, p2, p3, gt, S):
    _, B, H, T, _ = p1.shape
    n = T // CH
    n_lat = S // CH
    fwd = lambda i: (n_lat + i) % n
    bwd = lambda i: n - 1 - i
    specs = []
    for d, order in ((0, fwd), (1, bwd)):
        for w, rows in ((2 * CH, CH), (2 * CH, CH), (CH, CH), (LANE, 8)):
            specs.append(pl.BlockSpec((1, B, H, rows, w), functools.partial(lambda i, d, order: (d, 0, 0, order(i), 0), d=d, order=order)))
    out = jax.ShapeDtypeStruct((B, H, T, GDN_DV), F32)
    return pl.pallas_call(
        _gdn_scan_kernel,
        grid=(n,),
        in_specs=specs,
        out_specs=[pl.BlockSpec((B, H, CH, GDN_DV), lambda i: (0, 0, fwd(i), 0)),
                   pl.BlockSpec((B, H, CH, GDN_DV), lambda i: (0, 0, bwd(i), 0))],
        out_shape=[out, out],
        scratch_shapes=[pltpu.VMEM((2, B, H, GDN_DK, GDN_DV), F32)],
        compiler_params=_cparams(("arbitrary",)),
        name="gdn_scan",
    )(p1, p2, p3, gt, p1, p2, p3, gt)


def _rms(v, gain):
    return v * lax.rsqrt(jnp.mean(v * v, axis=-1, keepdims=True) + EPS) * gain


def _mix_out_kernel(x_ref, mod_ref, oa_ref, of_ref, ob_ref, z_ref, oc_ref, gw_ref, wo_ref,
                    post_ref, pre2_ref, rw_ref, rb_ref,
                    x1_ref, h2_ref, wd_ref, pos_ref, cum_ref, carry_ref, *, tiles_per_super):
    g = pl.program_id(0) * pl.num_programs(1) + pl.program_id(1)

    @pl.when(g % tiles_per_super == 0)
    def _reset():
        carry_ref[...] = jnp.zeros_like(carry_ref)

    z = z_ref[0]
    gated = []
    for h in range(GDN_HEADS):
        o = of_ref[0, h] + ob_ref[0, h]
        zh = z[:, GDN_DV * h:GDN_DV * (h + 1)]
        gated.append(_rms(o, gw_ref[...]) * (zh * jax.nn.sigmoid(zh)))
    mixed = jnp.concatenate([oa_ref[0], jnp.concatenate(gated, axis=1).astype(BF16), oc_ref[0]], axis=1)
    y = jnp.dot(mixed, wo_ref[...], preferred_element_type=F32)
    x1 = x_ref[0] + mod_ref[0, 2:3, :] * _rms(y, post_ref[...])
    x1_ref[0] = x1
    h2 = (_rms(x1, pre2_ref[...]) * (1.0 + mod_ref[0, 4:5, :]) + mod_ref[0, 3:4, :]).astype(BF16)
    h2_ref[...] = h2

    scores = jax.nn.sigmoid(_nt_dot(rw_ref[...], h2))
    sel = scores + rb_ref[...]
    gsz = N_EXPERTS // MOE_GROUPS
    g3 = sel.reshape(MOE_GROUPS, gsz, TM)
    io = lax.broadcasted_iota(jnp.int32, g3.shape, 1)
    m1 = jnp.max(g3, axis=1, keepdims=True)
    i1 = jnp.min(jnp.where(g3 == m1, io, gsz), axis=1, keepdims=True)
    m2 = jnp.max(jnp.where(io == i1, -jnp.inf, g3), axis=1, keepdims=True)

    def top_mask(vals, k):
        n = vals.shape[0]
        idx = lax.broadcasted_iota(jnp.int32, vals.shape, 0)
        mask = jnp.zeros(vals.shape, F32)
        for _ in range(k):
            mx = jnp.max(vals, axis=0, keepdims=True)
            hit = idx == jnp.min(jnp.where(vals == mx, idx, n), axis=0, keepdims=True)
            mask = jnp.where(hit, 1.0, mask)
            vals = jnp.where(hit, -jnp.inf, vals)
        return mask

    gmask = top_mask((m1 + m2).reshape(MOE_GROUPS, TM), MOE_TOPK_GROUPS)
    masked = jnp.where(gmask.reshape(MOE_GROUPS, 1, TM) > 0.0, g3, -jnp.inf).reshape(N_EXPERTS, TM)
    smask = top_mask(masked, MOE_TOP_K)
    w = jnp.where(smask > 0.0, scores, 0.0)
    wn = w / jnp.sum(w, axis=0, keepdims=True) * ROUTED_SCALE
    wd_ref[...] = jnp.where(smask > 0.0, wn, -1.0)

    ci = lax.broadcasted_iota(jnp.int32, (TM, TM), 0)
    cj = lax.broadcasted_iota(jnp.int32, (TM, TM), 1)
    ut = jnp.where(ci <= cj, 1.0, 0.0).astype(BF16)
    cs = jnp.dot(smask.astype(BF16), ut, preferred_element_type=F32)
    carry = carry_ref[...]
    pos_ref[...] = jnp.where(smask > 0.0, carry + cs - 1.0, -1.0).astype(jnp.int32)
    carry = carry + jnp.broadcast_to(cs[:, TM - 1:TM], (N_EXPERTS, TM))
    carry_ref[...] = carry
    cum_ref[0] = carry[:, :LANE]


def _mix_out(xx, modl, oa, o_f, o_b, z, oc, gw, wo, post, pre2, rw, rb, n_lat, nt, tiles_per_super):
    B, T, D = xx.shape
    nb = modl.shape[0] - 1
    N = B * nt * TM

    def mod_map(b, t):
        return (jnp.where(t >= n_lat, nb, b), 0, 0)

    row = lambda w: pl.BlockSpec((1, TM, w), lambda b, t: (b, t, 0))
    flat = lambda b, t: (0, b * nt + t)
    kern = functools.partial(_mix_out_kernel, tiles_per_super=tiles_per_super)
    return pl.pallas_call(
        kern,
        grid=(B, nt),
        in_specs=[row(D), pl.BlockSpec((1, 6, D), mod_map), row(384),
                  pl.BlockSpec((1, GDN_HEADS, TM, GDN_DV), lambda b, t: (b, 0, t, 0)),
                  pl.BlockSpec((1, GDN_HEADS, TM, GDN_DV), lambda b, t: (b, 0, t, 0)),
                  row(384), row(256), _const_spec((1, GDN_DV)), _const_spec(wo.shape),
                  _const_spec((1, D)), _const_spec((1, D)), _const_spec(rw.shape), _const_spec((N_EXPERTS, 1))],
        out_specs=[row(D),
                   pl.BlockSpec((TM, D), lambda b, t: (b * nt + t, 0)),
                   pl.BlockSpec((N_EXPERTS, TM), flat),
                   pl.BlockSpec((N_EXPERTS, TM), flat),
                   pl.BlockSpec((1, N_EXPERTS, LANE), lambda b, t: (b * nt + t, 0, 0))],
        out_shape=[jax.ShapeDtypeStruct((B, nt * TM, D), F32),
                   jax.ShapeDtypeStruct((N, D), BF16),
                   jax.ShapeDtypeStruct((N_EXPERTS, N), F32),
                   jax.ShapeDtypeStruct((N_EXPERTS, N), jnp.int32),
                   jax.ShapeDtypeStruct((N // TM, N_EXPERTS, LANE), F32)],
        scratch_shapes=[pltpu.VMEM((N_EXPERTS, TM), F32)],
        compiler_params=_cparams(("arbitrary", "arbitrary")),
        name="mix_out_router",
    )(xx, modl, oa, o_f, o_b, z, oc, gw, wo, post, pre2, rw, rb)


MOE_BLOCK = 128


def _moe_meta(pos, cum, N, ST):
    tps, ns = ST // TM, N // ST
    npair = ns * N_EXPERTS
    cumt = cum[:, :, 0].astype(jnp.int32).reshape(ns, tps, N_EXPERTS)
    cnt = cumt[:, -1, :].reshape(-1)
    nblk = (cnt + MOE_BLOCK - 1) // MOE_BLOCK
    bend = jnp.cumsum(nblk)
    bstart = bend - nblk
    total = bend[-1]
    n_blocks = (N * MOE_TOP_K) // MOE_BLOCK + npair
    bid = jnp.arange(n_blocks, dtype=jnp.int32)
    valid = bid < total
    pair = jnp.minimum(jnp.searchsorted(bend, jnp.minimum(bid, total - 1), side='right'), npair - 1).astype(jnp.int32)
    s_b, e_b = pair // N_EXPERTS, pair % N_EXPERTS
    k0 = (bid - bstart[pair]) * MOE_BLOCK
    kend = jnp.minimum(k0 + MOE_BLOCK, cnt[pair])
    ct = cumt[s_b, :, e_b]
    lo = jnp.sum(ct <= k0[:, None], axis=1).astype(jnp.int32)
    hi = jnp.sum(ct < kend[:, None], axis=1).astype(jnp.int32)
    hi = jnp.where(valid, hi, lo - 1)
    first_of_super = bstart.reshape(ns, N_EXPERTS)[:, 0]
    first = jnp.logical_and(valid, bid == first_of_super[s_b]).astype(jnp.int32)
    rowstart = (bstart * MOE_BLOCK).reshape(ns, N_EXPERTS).T
    dest = jnp.where(pos >= 0, pos + jnp.repeat(rowstart, ST, axis=1), -1)
    return (s_b, e_b, lo, hi, first, valid.astype(jnp.int32)), dest, n_blocks


def _moe_kernel(s_ref, e_ref, lo_ref, hi_ref, first_ref, valid_ref,
                h2_ref, dest_ref, wd_ref, wg_ref, wu_ref, wdn_ref, out_ref, xg_ref, wc_ref):
    b = pl.program_id(0)

    @pl.when(first_ref[b] == 1)
    def _zero():
        out_ref[...] = jnp.zeros_like(out_ref)

    @pl.when(valid_ref[b] == 1)
    def _block():
        lo, hi = lo_ref[b], hi_ref[b]
        rows = b * MOE_BLOCK + lax.broadcasted_iota(jnp.int32, (MOE_BLOCK, TM), 0)

        def onehot(j):
            off = pl.multiple_of(j * TM, TM)
            return rows == dest_ref[0, :, pl.ds(off, TM)], off

        xg_ref[...] = jnp.zeros_like(xg_ref)
        wc_ref[...] = jnp.zeros_like(wc_ref)

        def gather(j, c):
            hit, off = onehot(j)
            xg_ref[...] += jnp.dot(jnp.where(hit, 1.0, 0.0).astype(BF16), h2_ref[pl.ds(off, TM), :],
                                   preferred_element_type=F32)
            wc_ref[...] += jnp.sum(jnp.where(hit, wd_ref[0, :, pl.ds(off, TM)], 0.0), axis=1, keepdims=True)
            return c

        lax.fori_loop(lo, hi + 1, gather, 0)
        xb = xg_ref[...].astype(BF16)
        hg = jnp.dot(xb, wg_ref[0], preferred_element_type=F32)
        hu = jnp.dot(xb, wu_ref[0], preferred_element_type=F32)
        y = jnp.dot((hg * jax.nn.sigmoid(hg) * hu).astype(BF16), wdn_ref[0], preferred_element_type=F32)
        yw = (y * wc_ref[:, 0:1]).astype(BF16)

        def scatter(j, c):
            hit, off = onehot(j)
            out_ref[pl.ds(off, TM), :] += lax.dot_general(jnp.where(hit, 1.0, 0.0).astype(BF16), yw,
                                                          (((0,), (0,)), ((), ())), preferred_element_type=F32)
            return c

        lax.fori_loop(lo, hi + 1, scatter, 0)


def _moe_routed(h2, wd, dest, meta, n_blocks, wg, wu, wdn, ST):
    N, D = h2.shape
    E = N_EXPERTS
    grid_spec = pltpu.PrefetchScalarGridSpec(
        num_scalar_prefetch=6,
        grid=(n_blocks,),
        in_specs=[pl.BlockSpec((ST, D), lambda b, s, e, *_: (s[b], 0)),
                  pl.BlockSpec((1, 1, ST), lambda b, s, e, *_: (e[b], 0, s[b])),
                  pl.BlockSpec((1, 1, ST), lambda b, s, e, *_: (e[b], 0, s[b])),
                  pl.BlockSpec((1, D, D_EXPERT), lambda b, s, e, *_: (e[b], 0, 0)),
                  pl.BlockSpec((1, D, D_EXPERT), lambda b, s, e, *_: (e[b], 0, 0)),
                  pl.BlockSpec((1, D_EXPERT, D), lambda b, s, e, *_: (e[b], 0, 0))],
        out_specs=pl.BlockSpec((ST, D), lambda b, s, e, *_: (s[b], 0)),
        scratch_shapes=[pltpu.VMEM((MOE_BLOCK, D), F32), pltpu.VMEM((MOE_BLOCK, LANE), F32)],
    )
    return pl.pallas_call(
        _moe_kernel,
        grid_spec=grid_spec,
        out_shape=jax.ShapeDtypeStruct((N, D), F32),
        compiler_params=_cparams(("arbitrary",)),
        name="moe_routed",
    )(*meta, h2, dest.reshape(E, 1, N), wd.reshape(E, 1, N), wg, wu, wdn)


def _ffn_out_kernel(x1_ref, mod_ref, h2_ref, routed_ref, sg_ref, su_ref, sd_ref, post_ref, o_ref):
    h2 = h2_ref[...]
    hg = jnp.dot(h2, sg_ref[...], preferred_element_type=F32)
    hu = jnp.dot(h2, su_ref[...], preferred_element_type=F32)
    f = jnp.dot((hg * jax.nn.sigmoid(hg) * hu).astype(BF16), sd_ref[...], preferred_element_type=F32) + routed_ref[...]
    o_ref[0] = x1_ref[0] + mod_ref[0, 5:6, :] * _rms(f, post_ref[...])


def _ffn_out(x1, modl, h2, routed, sg, su, sd, post, n_lat):
    B, Tn, D = x1.shape
    nt = Tn // TM
    nb = modl.shape[0] - 1

    def mod_map(b, t):
        return (jnp.where(t >= n_lat, nb, b), 0, 0)

    row = pl.BlockSpec((1, TM, D), lambda b, t: (b, t, 0))
    flat = pl.BlockSpec((TM, D), lambda b, t: (b * nt + t, 0))
    return pl.pallas_call(
        _ffn_out_kernel,
        grid=(B, nt),
        in_specs=[row, pl.BlockSpec((1, 6, D), mod_map), flat, flat,
                  _const_spec(sg.shape), _const_spec(su.shape), _const_spec(sd.shape), _const_spec((1, D))],
        out_specs=row,
        out_shape=jax.ShapeDtypeStruct((B, Tn, D), F32),
        compiler_params=_cparams(("parallel", "parallel")),
        name="ffn_out",
    )(x1, modl, h2, routed, sg, su, sd, post)


def _tiles_per_super(n_tiles):
    return max(k for k in range(1, 13) if n_tiles % k == 0)


def kernel(x, c, ctx, c_ctx, ada_w, ada_b, mix_norm_pre, mix_norm_post, ffn_norm_pre, ffn_norm_post, w_in, w_out, swa_sink, gdn_conv_w, gdn_a_log, gdn_dt_bias, gdn_norm_w, mla_q_norm, mla_w_uq, mla_kv_norm, mla_w_ukv, router_w, router_bias, expert_w_gate, expert_w_up, expert_w_down, shared_w_gate, shared_w_up, shared_w_down):
    B, S, D = x.shape
    L = ctx.shape[1]
    assert L == TM and S % (2 * TM) == 0
    T = S + L
    n_lat = S // TM
    xx = jnp.concatenate([x, ctx], axis=1)
    tabs = _rope_tables(S, T)
    cvecs = jnp.concatenate([c, c_ctx[None], jnp.zeros((8 - B - 1, D), F32)], axis=0)
    mods = _modulation(cvecs, ada_w, ada_b).reshape(DEPTH, 8, 6, D)[:, :B + 1]
    for layer in range(DEPTH):
        xx = _layer(layer, xx, mods[layer], tabs, S, layer == DEPTH - 1,
                    mix_norm_pre, mix_norm_post, ffn_norm_pre, ffn_norm_post, w_in, w_out, swa_sink, gdn_conv_w,
                    gdn_a_log, gdn_dt_bias, gdn_norm_w, mla_q_norm, mla_w_uq, mla_kv_norm, mla_w_ukv, router_w,
                    router_bias, expert_w_gate, expert_w_up, expert_w_down, shared_w_gate, shared_w_up, shared_w_down)
    return xx


def _layer(layer, xx, modl, tabs, S, last, mix_norm_pre, mix_norm_post, ffn_norm_pre, ffn_norm_post, w_in, w_out,
           swa_sink, gdn_conv_w, gdn_a_log, gdn_dt_bias, gdn_norm_w, mla_q_norm, mla_w_uq, mla_kv_norm, mla_w_ukv,
           router_w, router_bias, expert_w_gate, expert_w_up, expert_w_down, shared_w_gate, shared_w_up, shared_w_down):
    B, T, D = xx.shape
    n_lat = S // TM
    nt = n_lat if last else T // TM
    qa, ka, va, gx, z, ba, qm, km, vm = _project(
        xx, modl, mix_norm_pre[layer][None], _prep_w_in(w_in[layer]), _prep_w_uq(mla_w_uq[layer]),
        _prep_w_ukv(mla_w_ukv[layer]), jnp.pad(mla_q_norm[layer], (0, 256 - MLA_Q_RANK))[None],
        mla_kv_norm[layer][None], tabs, n_lat)
    oa = _swa_attention(swa_sink[layer], qa, ka, va, S, nt)
    oc = _mla_attention(qm, km, vm, S, nt)
    gp = jnp.zeros((2, LANE), F32)
    g0 = _BA_LANE + 2 * GDN_HEADS
    gp = gp.at[0, g0:g0 + 2 * GDN_HEADS].set(gdn_a_log[layer].reshape(-1))
    gp = gp.at[1, g0:g0 + 2 * GDN_HEADS].set(gdn_dt_bias[layer].reshape(-1))
    p1, p2, p3, gt = _gdn_local(gx, gdn_conv_w[layer], ba, gp, n_lat)
    o_f, o_b = _gdn_scan(p1, p2, p3, gt, S)

    tps = _tiles_per_super(B * nt)
    ST = tps * TM
    N = B * nt * TM
    x1, h2, wd, pos, cum = _mix_out(
        xx, modl, oa, o_f, o_b, z, oc, gdn_norm_w[layer][None], _prep_w_out(w_out[layer]),
        mix_norm_post[layer][None], ffn_norm_pre[layer][None], router_w[layer].T.astype(BF16),
        router_bias[layer][:, None], n_lat, nt, tps)
    meta, dest, n_blocks = _moe_meta(pos, cum, N, ST)
    routed = _moe_routed(h2, wd, dest, meta, n_blocks, expert_w_gate[layer].astype(BF16),
                         expert_w_up[layer].astype(BF16), expert_w_down[layer].astype(BF16), ST)
    return _ffn_out(x1, modl, h2, routed, shared_w_gate[layer].astype(BF16), shared_w_up[layer].astype(BF16),
                    shared_w_down[layer].astype(BF16), ffn_norm_post[layer][None], n_lat)
```

```python
import functools
import math

import numpy as np
import jax
import jax.numpy as jnp
from jax import lax
from jax.experimental import pallas as pl
from jax.experimental.pallas import tpu as pltpu

F32 = jnp.float32
BF16 = jnp.bfloat16

DEPTH = 2
GRID_W = 64
EPS = 1e-6
ROPE_BASE = 10000.0
HEAD_DIM = 64
SWA_HEADS = 6
SWA_KV_HEADS = 2
SWA_WINDOW = 128
GDN_HEADS = 6
GDN_DK = 64
GDN_DV = 64
GDN_CONV = 5
GDN_CHUNK = 64
MLA_HEADS = 4
MLA_Q_RANK = 192
MLA_KV_RANK = 128
MLA_NOPE = 64
MLA_ROPE = 32
MLA_V = 64
N_EXPERTS = 64
MOE_TOP_K = 8
MOE_GROUPS = 8
MOE_TOPK_GROUPS = 4
D_EXPERT = 256
ROUTED_SCALE = 2.5
LOG2_E = math.log2(math.e)

_SPLITS = (384, 128, 128, 1152, 384, 24, 192, 128, 32)
_OFF = np.concatenate([[0], np.cumsum(_SPLITS)]).tolist()
D_PROJ = _OFF[-1]

TM = 256
LANE = 128
VMEM_LIMIT = 56 * 1024 * 1024

_C_QA, _C_QAS, _C_KA, _C_KAS, _C_VA, _C_GDN, _C_Z, _C_CQ, _C_CKV, _C_KRB, _C_END = (
    0, 768, 1536, 1664, 1792, 1920, 3072, 3456, 3712, 3840, 3968)
_BA_LANE = 32


def _cparams(sem, vmem=VMEM_LIMIT):
    return pltpu.CompilerParams(dimension_semantics=sem, vmem_limit_bytes=vmem)


def _rope_partner(d, width):
    half, n = width // 2, width // 4
    i = d % half
    return (d // half) * half + (i + n if i < n else i - n)


def _take_cols(w, idx):
    idx = np.asarray(idx)
    wz = jnp.concatenate([w, jnp.zeros((w.shape[0], 1), w.dtype)], axis=1)
    return jnp.take(wz, jnp.asarray(np.where(idx < 0, w.shape[1], idx)), axis=1)


def _prep_w_in(w_in):
    cols = []
    for swap in (False, True):
        for h in range(SWA_HEADS):
            j = h // (SWA_HEADS // SWA_KV_HEADS)
            blk = [-1] * LANE
            for d in range(HEAD_DIM):
                blk[64 * j + d] = _OFF[0] + h * HEAD_DIM + (_rope_partner(d, HEAD_DIM) if swap else d)
            cols += blk
    for swap in (False, True):
        for j in range(SWA_KV_HEADS):
            cols += [_OFF[1] + j * HEAD_DIM + (_rope_partner(d, HEAD_DIM) if swap else d) for d in range(HEAD_DIM)]
    cols += list(range(_OFF[2], _OFF[3]))
    cols += list(range(_OFF[3], _OFF[4]))
    cols += list(range(_OFF[4], _OFF[5]))
    cols += list(range(_OFF[6], _OFF[7])) + [-1] * 64
    cols += list(range(_OFF[7], _OFF[8]))
    cols += list(range(_OFF[8], _OFF[9])) + list(range(_OFF[5], _OFF[6])) + [-1] * (LANE - 32 - 24)
    assert len(cols) == _C_END
    return _take_cols(w_in, cols).astype(BF16)


def _prep_w_uq(w_uq):
    cols = []
    for swap in (False, True):
        for h in range(MLA_HEADS):
            base = h * (MLA_NOPE + MLA_ROPE)
            blk = [-1] * LANE
            for d in range(MLA_NOPE):
                blk[d] = -1 if swap else base + d
            for r in range(MLA_ROPE):
                blk[MLA_NOPE + r] = base + MLA_NOPE + (_rope_partner(r, MLA_ROPE) if swap else r)
            cols += blk
    w = _take_cols(w_uq, cols)
    return jnp.pad(w, ((0, 256 - MLA_Q_RANK), (0, 0))).astype(BF16)


def _prep_w_ukv(w_ukv):
    kcols, vcols = [], []
    for h in range(MLA_HEADS):
        base = h * (MLA_NOPE + MLA_V)
        kcols += [base + d for d in range(MLA_NOPE)] + [-1] * 64
        vcols += [base + MLA_NOPE + d for d in range(MLA_V)]
    wk = _take_cols(w_ukv, kcols)
    wv = _take_cols(w_ukv, vcols)
    top = jnp.concatenate([wk, jnp.zeros_like(wk), wv], axis=1)
    place = np.zeros((128, 1280), np.float32)
    for h in range(MLA_HEADS):
        for r in range(MLA_ROPE):
            place[r, 128 * h + MLA_NOPE + r] = 1.0
            place[_rope_partner(r, MLA_ROPE), 512 + 128 * h + MLA_NOPE + r] = 1.0
    return jnp.concatenate([top, jnp.asarray(place)], axis=0).astype(BF16)


def _prep_w_out(w_out):
    rows = []
    G = SWA_HEADS // SWA_KV_HEADS
    for g in range(G):
        for j in range(SWA_KV_HEADS):
            rows += [(G * j + g) * HEAD_DIM + d for d in range(HEAD_DIM)]
    rows += list(range(SWA_HEADS * HEAD_DIM, w_out.shape[0]))
    return jnp.take(w_out, jnp.asarray(rows), axis=0).astype(BF16)


def _rope_tables(S, T):
    t = np.arange(S)
    row, col = t // GRID_W, t % GRID_W

    def tab(width, lanes):
        half, n = width // 2, width // 4
        c = np.ones((T, LANE), np.float64)
        s = np.zeros((T, LANE), np.float64)
        for lane, d in lanes:
            i = d % half
            pos = row if d < half else col
            ang =(pos.astype(np.float32) * np.float32(ROPE_BASE ** (-(i % n) / n))).astype(np.float64)
            c[:S, lane] = np.cos(ang)
            s[:S, lane] = -np.sin(ang) if i < n else np.sin(ang)
        return jnp.asarray(c, F32), jnp.asarray(s, F32)

    ca, sa = tab(HEAD_DIM, [(l, l % HEAD_DIM) for l in range(LANE)])
    cm, sm = tab(MLA_ROPE, [(MLA_NOPE + r, r) for r in range(MLA_ROPE)])
    return ca, sa, cm, sm


def _mod_kernel(c_ref, w_ref, b_ref, o_ref):
    cv = c_ref[...]
    a = (cv * jax.nn.sigmoid(cv)).astype(BF16)
    o_ref[0] = jnp.dot(a, w_ref[0].astype(BF16), preferred_element_type=F32) + b_ref[0]


def _modulation(cvecs, ada_w, ada_b):
    depth, D, N = ada_w.shape
    tn = 512
    return pl.pallas_call(
        _mod_kernel,
        grid=(depth, N // tn),
        in_specs=[pl.BlockSpec((8, D), lambda l, j: (0, 0)),
                  pl.BlockSpec((1, D, tn), lambda l, j: (l, 0, j)),
                  pl.BlockSpec((1, 1, tn), lambda l, j: (l, 0, j))],
        out_specs=pl.BlockSpec((1, 8, tn), lambda l, j: (l, 0, j)),
        out_shape=jax.ShapeDtypeStruct((depth, 8, N), F32),
        compiler_params=_cparams(("arbitrary", "arbitrary")),
        name="modulation",
    )(cvecs, ada_w, ada_b.reshape(depth, 1, N))


def _proj_kernel(x_ref, mod_ref, gain_ref, w_ref, wq2_ref, wk2_ref, qg_ref, kvg_ref,
                 ca_ref, sa_ref, cm_ref, sm_ref,
                 qa_ref, ka_ref, va_ref, gx_ref, z_ref, ba_ref, qm_ref, km_ref, vm_ref):
    x = x_ref[0]
    ms = jnp.mean(x * x, axis=-1, keepdims=True)
    h = x * lax.rsqrt(ms + EPS) * gain_ref[...]
    h = h * (1.0 + mod_ref[0, 1:2, :]) + mod_ref[0, 0:1, :]
    hb = h.astype(BF16)

    ca, sa, cm, sm = ca_ref[...], sa_ref[...], cm_ref[...], sm_ref[...]

    def rope(a, b, c, s):
        n = a.shape[1] // LANE
        return a * jnp.concatenate([c] * n, axis=1) + b * jnp.concatenate([s] * n, axis=1)

    p1 = jnp.dot(hb, w_ref[:, _C_QA:_C_GDN], preferred_element_type=F32)
    qa = rope(p1[:, _C_QA:_C_QAS], p1[:, _C_QAS:_C_KA], ca, sa)
    qa_ref[0] = (qa * (HEAD_DIM ** -0.5)).astype(BF16)
    ka_ref[0] = rope(p1[:, _C_KA:_C_KAS], p1[:, _C_KAS:_C_VA], ca, sa).astype(BF16)
    va_ref[0] = p1[:, _C_VA:_C_GDN].astype(BF16)

    p2 = jnp.dot(hb, w_ref[:, _C_GDN:_C_Z], preferred_element_type=F32)
    for j in range(3 * GDN_HEADS):
        gx_ref[0, j] = p2[:, 64 * j:64 * j + 64]

    p3 = jnp.dot(hb, w_ref[:, _C_Z:_C_END], preferred_element_type=F32)
    z_ref[0] = p3[:, 0:_C_CQ - _C_Z]
    cq = p3[:, _C_CQ - _C_Z:_C_CKV - _C_Z]
    ckv = p3[:, _C_CKV - _C_Z:_C_KRB - _C_Z]
    krb = p3[:, _C_KRB - _C_Z:]
    ba_ref[0] = krb

    cqn = cq * lax.rsqrt(jnp.sum(cq * cq, axis=-1, keepdims=True) * (1.0 / MLA_Q_RANK) + EPS) * qg_ref[...]
    e = jnp.dot(cqn.astype(BF16), wq2_ref[...], preferred_element_type=F32)
    qm = rope(e[:, :512], e[:, 512:], cm, sm)
    qm_ref[0] = (qm * (LOG2_E * (MLA_NOPE + MLA_ROPE) ** -0.5)).astype(BF16)

    ckvn = ckv * lax.rsqrt(jnp.mean(ckv * ckv, axis=-1, keepdims=True) + EPS) * kvg_ref[...]
    lhs2 = jnp.concatenate([ckvn.astype(BF16), krb.astype(BF16)], axis=1)
    e2 = jnp.dot(lhs2, wk2_ref[...], preferred_element_type=F32)
    km_ref[0] = rope(e2[:, :512], e2[:, 512:1024], cm, sm).astype(BF16)
    vm_ref[0] = e2[:, 1024:].astype(BF16)


def _const_spec(shape):
    nd = len(shape)
    return pl.BlockSpec(shape, lambda *_: (0,) * nd)


def _project(xx, modl, gain, w_main, wq2, wk2, qg, kvg, tabs, n_lat_tiles):
    B, T, D = xx.shape
    nt = T // TM
    nb = modl.shape[0] - 1

    def mod_map(b, t):
        return (jnp.where(t >= n_lat_tiles, nb, b), 0, 0)

    row = lambda w: pl.BlockSpec((1, TM, w), lambda b, t: (b, t, 0))
    tab = pl.BlockSpec((TM, LANE), lambda b, t: (t, 0))
    out_shapes = [
        jax.ShapeDtypeStruct((B, T, 768), BF16),
        jax.ShapeDtypeStruct((B, T, 128), BF16),
        jax.ShapeDtypeStruct((B, T, 128), BF16),
        jax.ShapeDtypeStruct((B, 18, T, 64), F32),
        jax.ShapeDtypeStruct((B, T, 384), F32),
        jax.ShapeDtypeStruct((B, T, 128), F32),
        jax.ShapeDtypeStruct((B, T, 512), BF16),
        jax.ShapeDtypeStruct((B, T, 512), BF16),
        jax.ShapeDtypeStruct((B, T, 256), BF16),
    ]
    out_specs = [row(768), row(128), row(128),
                 pl.BlockSpec((1, 18, TM, 64), lambda b, t: (b, 0, t, 0)),
                 row(384), row(128), row(512), row(512), row(256)]
    return pl.pallas_call(
        _proj_kernel,
        grid=(B, nt),
        in_specs=[row(D), pl.BlockSpec((1, 6, D), mod_map), _const_spec((1, D)),
                  _const_spec(w_main.shape), _const_spec(wq2.shape), _const_spec(wk2.shape),
                  _const_spec((1, 256)), _const_spec((1, 128)), tab, tab, tab, tab],
        out_specs=out_specs,
        out_shape=out_shapes,
        compiler_params=_cparams(("parallel", "parallel")),
        name="in_proj",
    )(xx, modl, gain, w_main, wq2, wk2, qg, kvg, *tabs)


def _nt_dot(a, b):
    return lax.dot_general(a, b, (((1,), (1,)), ((), ())), preferred_element_type=F32)


def _swa_kernel(sink_ref, q_ref, k_ref, v_ref, o_ref, *, S, n_lat):
    i = pl.program_id(1)
    G = SWA_HEADS // SWA_KV_HEADS
    W = 2 * TM
    lane = lax.broadcasted_iota(jnp.int32, (TM, LANE), 1)
    kc = k_ref[0, pl.ds(S, TM), :]
    vc = v_ref[0, pl.ds(S, TM), :]

    def heads(local):
        outs = []
        for g in range(G):
            og = []
            for j in range(SWA_KV_HEADS):
                h = G * j + g
                q = q_ref[0, :, LANE * h:LANE * (h + 1)]
                sink = sink_ref[h]
                s_ctx = _nt_dot(q, kc)
                m = jnp.maximum(jnp.max(s_ctx, axis=-1, keepdims=True), sink)
                if local is not None:
                    kw, vw, valid = local
                    s_loc = jnp.where(valid, _nt_dot(q, kw), -jnp.inf)
                    m = jnp.maximum(m, jnp.max(s_loc, axis=-1, keepdims=True))
                p_ctx = jnp.exp(s_ctx - m)
                den = jnp.sum(p_ctx, axis=-1, keepdims=True) + jnp.exp(sink - m)
                o = jnp.dot(p_ctx.astype(BF16), vc, preferred_element_type=F32)
                if local is not None:
                    p_loc = jnp.exp(s_loc - m)
                    den = den + jnp.sum(p_loc, axis=-1, keepdims=True)
                    o = o + jnp.dot(p_loc.astype(BF16), vw, preferred_element_type=F32)
                og.append(o * (1.0 / den))
            outs.append(jnp.where(lane < HEAD_DIM, og[0], og[1]))
        o_ref[0] = jnp.concatenate(outs, axis=1).astype(BF16)

    @pl.when(i < n_lat)
    def _latent():
        start = pl.multiple_of(jnp.clip(i * TM - SWA_WINDOW, 0, S - W), LANE)
        kw = k_ref[0, pl.ds(start, W), :]
        vw = v_ref[0, pl.ds(start, W), :]
        qpos = i * TM + lax.broadcasted_iota(jnp.int32, (TM, W), 0)
        kpos = start + lax.broadcasted_iota(jnp.int32, (TM, W), 1)
        heads((kw, vw, jnp.abs(qpos - kpos) <= SWA_WINDOW))

    @pl.when(i >= n_lat)
    def _context():
        heads(None)


def _swa_attention(sink, qa, ka, va, S, nt):
    B, T, _ = qa.shape
    kern = functools.partial(_swa_kernel, S=S, n_lat=S // TM)
    return pl.pallas_call(
        kern,
        grid=(B, nt),
        in_specs=[pl.BlockSpec(memory_space=pltpu.SMEM),
                  pl.BlockSpec((1, TM, 768), lambda b, t: (b, t, 0)),
                  pl.BlockSpec((1, T, 128), lambda b, t: (b, 0, 0)),
                  pl.BlockSpec((1, T, 128), lambda b, t: (b, 0, 0))],
        out_specs=pl.BlockSpec((1, TM, 384), lambda b, t: (b, t, 0)),
        out_shape=jax.ShapeDtypeStruct((B, nt * TM, 384), BF16),
        compiler_params=_cparams(("parallel", "parallel")),
        name="swa_attention",
    )(sink, qa, ka, va)


MLA_SUBSTEPS = 2


def _mla_kernel(q_ref, k_ref, v_ref, o_ref, *, S, n_lat):
    i = pl.program_id(1)
    lane = lax.broadcasted_iota(jnp.int32, (TM, LANE), 1)
    MLA_KV_CHUNK = max(c for c in (4096, 2048, 1024, 512) if S % c == 0)
    n_chunks = jnp.where(i < n_lat, S // MLA_KV_CHUNK, 0)

    def lane_blocks(s):
        return [s[:, LANE * j:LANE * (j + 1)] for j in range(s.shape[1] // LANE)]

    outs = []
    for h in range(MLA_HEADS):
        q = q_ref[0, :, LANE * h:LANE * (h + 1)]

        def scores(off, size):
            return _nt_dot(q, k_ref[0, pl.ds(off, size), LANE * h:LANE * (h + 1)])

        def values(off, size):
            return v_ref[0, pl.ds(off, size), LANE * (h // 2):LANE * (h // 2 + 1)]

        def step(off, size, carry):
            m, l, acc = carry
            blocks = lane_blocks(scores(off, size))
            mx = blocks[0]
            for blk in blocks[1:]:
                mx = jnp.maximum(mx, blk)
            m_new = jnp.maximum(m, jnp.broadcast_to(jnp.max(mx, axis=-1, keepdims=True), (TM, LANE)))
            alpha = jnp.exp2(m - m_new)
            ps = [jnp.exp2(blk - m_new) for blk in blocks]
            l = alpha * l
            for p in ps:
                l = l + p
            p = jnp.concatenate(ps, axis=1).astype(BF16)
            return m_new, l, alpha * acc + jnp.dot(p, values(off, size), preferred_element_type=F32)

        def body(c, cr):
            off = pl.multiple_of(c * MLA_KV_CHUNK, MLA_KV_CHUNK)
            for u in range(MLA_SUBSTEPS):
                cr = step(pl.multiple_of(off + u * sub, TM), sub, cr)
            return cr

        sub = MLA_KV_CHUNK // MLA_SUBSTEPS
        zero = jnp.zeros((TM, LANE), F32)
        carry = lax.fori_loop(0, n_chunks, body, (jnp.full((TM, LANE), -jnp.inf, F32), zero, zero))
        m, l, acc = step(S, TM, carry)
        outs.append(acc * (1.0 / jnp.sum(l, axis=-1, keepdims=True)))
    o_ref[0] = jnp.concatenate([jnp.where(lane < MLA_V, outs[0], outs[1]),
                                jnp.where(lane < MLA_V, outs[2], outs[3])], axis=1).astype(BF16)


def _mla_attention(qm, km, vm, S, nt):
    B, T, _ = qm.shape
    kern = functools.partial(_mla_kernel, S=S, n_lat=S // TM)
    return pl.pallas_call(
        kern,
        grid=(B, nt),
        in_specs=[pl.BlockSpec((1, TM, 512), lambda b, t: (b, t, 0)),
                  pl.BlockSpec((1, T, 512), lambda b, t: (b, 0, 0)),
                  pl.BlockSpec((1, T, 256), lambda b, t: (b, 0, 0))],
        out_specs=pl.BlockSpec((1, TM, 256), lambda b, t: (b, t, 0)),
        out_shape=jax.ShapeDtypeStruct((B, nt * TM, 256), BF16),
        compiler_params=_cparams(("parallel", "parallel")),
        name="mla_attention",
    )(qm, km, vm)


CH = GDN_CHUNK
NCH = TM // CH


def _split3(a):
    hi = a.astype(BF16)
    r = a - hi.astype(F32)
    mid = r.astype(BF16)
    return hi, mid, (r - mid.astype(F32)).astype(BF16)


def _bmm3(a, b):
    ah, al, _ = _split3(a)
    bh, bl, _ = _split3(b)
    f = lambda x, y: jnp.einsum('cij,cjk->cik', x, y, preferred_element_type=F32)
    return f(ah, bh) + (f(ah, bl) + f(al, bh))


GDN_HEADS_PER_STEP = 2


def _gdn_local_kernel(x_ref, xp_ref, xn_ref, cw_ref, ba_ref, gp_ref,
                      p1_ref, p2_ref, p3_ref, gt_ref, xs_ref, *, n_lat):
    t = pl.program_id(2)
    has_prev = jnp.logical_and(t > 0, t != n_lat)
    has_next = t < n_lat - 1

    baf = ba_ref[0]
    lane = lax.broadcasted_iota(jnp.int32, (TM, LANE), 1)
    beta_all = jax.nn.sigmoid(baf)
    xg = baf + gp_ref[1:2, :]
    g_all = -jnp.exp(gp_ref[0:1, :]) * (jnp.maximum(xg, 0.0) + jnp.log(1.0 + jnp.exp(-jnp.abs(xg))))

    def col(a, idx):
        cvec = jnp.sum(jnp.where(lane == idx, a, 0.0), axis=1, keepdims=True)
        return jnp.broadcast_to(cvec, (TM, CH)).reshape(NCH, CH, CH)

    ii = lax.broadcasted_iota(jnp.int32, (CH, CH), 0)
    jj = lax.broadcasted_iota(jnp.int32, (CH, CH), 1)
    eye = (ii == jj).astype(F32)[None]
    nt = lambda a, b: jnp.einsum('cid,cjd->cij', a, b, preferred_element_type=F32)

    for hh in range(GDN_HEADS_PER_STEP):
        h = pl.program_id(1) * GDN_HEADS_PER_STEP + hh
        parts = []
        for part in range(3):
            xs_ref[hh, part, 0:8, :] = jnp.where(has_prev, xp_ref[0, part, hh], 0.0)
            xs_ref[hh, part, 8:8 + TM, :] = x_ref[0, part, hh]
            xs_ref[hh, part, 8 + TM:16 + TM, :] = jnp.where(has_next, xn_ref[0, part, hh], 0.0)
            acc = jnp.zeros((TM, GDN_DK), F32)
            for k in range(GDN_CONV):
                acc = acc + cw_ref[hh, part, k:k + 1, :] * xs_ref[hh, part, pl.ds(8 - GDN_CONV // 2 + k, TM), :]
            parts.append(acc * jax.nn.sigmoid(acc))
        q_, k_, v = parts
        q = q_ * lax.rsqrt(jnp.sum(q_ * q_, axis=-1, keepdims=True) + EPS) * (GDN_DK ** -0.5)
        k = k_ * lax.rsqrt(jnp.sum(k_ * k_, axis=-1, keepdims=True) + EPS)
        q3, k3, v3 = (a.reshape(NCH, CH, GDN_DK) for a in (q, k, v))
        kb, qb = k3.astype(BF16), q3.astype(BF16)
        kk, qk = nt(kb, kb), nt(qb, kb)

        for d in range(2):
            beta = col(beta_all, _BA_LANE + GDN_HEADS * d + h)
            g = col(g_all, _BA_LANE + 2 * GDN_HEADS + GDN_HEADS * d + h)
            incl = (jj <= ii) if d == 0 else (jj >= ii)
            strict = (jj < ii) if d == 0 else (jj > ii)
            tri = jnp.broadcast_to(incl.astype(BF16)[None], (NCH, CH, CH))
            gcx = sum(jnp.einsum('cij,cjl->cil', tri, gpart, preferred_element_type=F32) for gpart in _split3(g))
            gcr = jnp.stack([gcx[c].T for c in range(NCH)], axis=0)
            decay = jnp.exp(jnp.where(incl[None], gcx - gcr, -jnp.inf))
            lm = jnp.where(strict[None], beta * kk * decay, 0.0)
            egc = jnp.exp(gcx)
            rhs = jnp.concatenate([v3 * beta, k3 * beta * egc], axis=-1)
            p = -lm
            tinv = eye + p
            for _ in range(5):
                p = _bmm3(p, p)
                tinv = tinv + _bmm3(tinv, p)
            uw = _bmm3(tinv, rhs)
            end = CH - 1 if d == 0 else 0
            gce = gcx[:, end:end + 1, :]
            k_out = k3 * jnp.exp(gce - gcx)
            p1_ref[d, 0, hh] = uw.reshape(TM, 2 * CH).astype(BF16)
            p2_ref[d, 0, hh] = jnp.concatenate([k_out, q3 * egc], axis=-1).reshape(TM, 2 * CH).astype(BF16)
            p3_ref[d, 0, hh] = (qk * decay).reshape(TM, CH).astype(BF16)
            gte = jnp.exp(gce)
            gt_ref[d, 0, hh] = jnp.broadcast_to(jnp.concatenate([gte, gte], axis=-1),
                                                (NCH, 8, LANE)).reshape(NCH * 8, LANE)


def _gdn_local(gx, conv_w, ba, gp, n_lat):
    B, _, T, _ = gx.shape
    H = GDN_HEADS
    hp = GDN_HEADS_PER_STEP
    nt = T // TM
    gx5 = gx.reshape(B, 3, H, T, GDN_DK)
    cw = jnp.transpose(conv_w.reshape(GDN_CONV, 3, H, GDN_DK), (2, 1, 0, 3))
    kern = functools.partial(_gdn_local_kernel, n_lat=n_lat)
    r8 = TM // 8
    big = lambda w, dt: jax.ShapeDtypeStruct((2, B, H, T, w), dt)
    ospec = lambda w: pl.BlockSpec((2, 1, hp, TM, w), lambda b, h, t: (0, b, h, t, 0))
    return pl.pallas_call(
        kern,
        grid=(B, H // hp, nt),
        in_specs=[pl.BlockSpec((1, 3, hp, TM, GDN_DK), lambda b, h, t: (b, 0, h, t, 0)),
                  pl.BlockSpec((1, 3, hp, 8, GDN_DK), lambda b, h, t: (b, 0, h, jnp.maximum(t * r8 - 1, 0), 0)),
                  pl.BlockSpec((1, 3, hp, 8, GDN_DK), lambda b, h, t: (b, 0, h, jnp.minimum((t + 1) * r8, T // 8 - 1), 0)),
                  pl.BlockSpec((hp, 3, GDN_CONV, GDN_DK), lambda b, h, t: (h, 0, 0, 0)),
                  pl.BlockSpec((1, TM, LANE), lambda b, h, t: (b, t, 0)),
                  pl.BlockSpec((2, LANE), lambda b, h, t: (0, 0))],
        out_specs=[ospec(2 * CH), ospec(2 * CH), ospec(CH),
                   pl.BlockSpec((2, 1, hp, NCH * 8, LANE), lambda b, h, t: (0, b, h, t, 0))],
        out_shape=[big(2 * CH, BF16), big(2 * CH, BF16), big(CH, BF16),
                   jax.ShapeDtypeStruct((2, B, H, (T // CH) * 8, LANE), F32)],
        scratch_shapes=[pltpu.VMEM((hp, 3, TM + 16, GDN_DK), F32)],
        compiler_params=_cparams(("parallel", "parallel", "parallel")),
        name="gdn_local",
    )(gx5, gx5, gx5, cw, ba, gp)


def _gdn_scan_kernel(p1f, p2f, p3f, gtf, p1b, p2b, p3b, gtb, of_ref, ob_ref, s_ref):
    @pl.when(pl.program_id(0) == 0)
    def _init():
        s_ref[...] = jnp.zeros_like(s_ref)

    nb, nh = s_ref.shape[1], s_ref.shape[2]
    for d, (p1, p2, p3, gt, o_ref) in enumerate(((p1f, p2f, p3f, gtf, of_ref), (p1b, p2b, p3b, gtb, ob_ref))):
        for b in range(nb):
            for h in range(nh):
                s = s_ref[d, b, h]
                uw, kq = p1[0, b, h], p2[0, b, h]
                m1 = jnp.dot(jnp.concatenate([uw[:, CH:], kq[:, CH:]], axis=0), s.astype(BF16),
                             preferred_element_type=F32)
                v_new = (uw[:, :CH].astype(F32) - m1[:CH]).astype(BF16)
                o_ref[b, h] = m1[CH:] + jnp.dot(p3[0, b, h], v_new, preferred_element_type=F32)
                ds = lax.dot_general(kq[:, :CH], v_new, (((0,), (0,)), ((), ())), preferred_element_type=F32)
                s_ref[d, b, h] = s * gt[0, b, h][0:1, :CH] + ds


def _gdn_scan(p1, p2, p3, gt, S):
    _, B, H, T, _ = p1.shape
    n = T // CH
    n_lat = S // CH
    fwd = lambda i: (n_lat + i) % n
    bwd = lambda i: n - 1 - i
    specs = []
    for d, order in ((0, fwd), (1, bwd)):
        for w, rows in ((2 * CH, CH), (2 * CH, CH), (CH, CH), (LANE, 8)):
            specs.append(pl.BlockSpec((1, B, H, rows, w), functools.partial(lambda i, d, order: (d, 0, 0, order(i), 0), d=d, order=order)))
    out = jax.ShapeDtypeStruct((B, H, T, GDN_DV), F32)
    return pl.pallas_call(
        _gdn_scan_kernel,
        grid=(n,),
        in_specs=specs,
        out_specs=[pl.BlockSpec((B, H, CH, GDN_DV), lambda i: (0, 0, fwd(i), 0)),
                   pl.BlockSpec((B, H, CH, GDN_DV), lambda i: (0, 0, bwd(i), 0))],
        out_shape=[out, out],
        scratch_shapes=[pltpu.VMEM((2, B, H, GDN_DK, GDN_DV), F32)],
        compiler_params=_cparams(("arbitrary",)),
        name="gdn_scan",
    )(p1, p2, p3, gt, p1, p2, p3, gt)


def _rms(v, gain):
    return v * lax.rsqrt(jnp.mean(v * v, axis=-1, keepdims=True) + EPS) * gain


def _mix_out_kernel(x_ref, mod_ref, oa_ref, of_ref, ob_ref, z_ref, oc_ref, gw_ref, wo_ref,
                    post_ref, pre2_ref, rw_ref, rb_ref,
                    x1_ref, h2_ref, wd_ref, pos_ref, rankt_ref, cum_ref, carry_ref, *, tiles_per_super):
    g = pl.program_id(0) * pl.num_programs(1) + pl.program_id(1)

    @pl.when(g % tiles_per_super == 0)
    def _reset():
        carry_ref[...] = jnp.zeros_like(carry_ref)

    z = z_ref[0]
    gated = []
    for h in range(GDN_HEADS):
        o = of_ref[0, h] + ob_ref[0, h]
        zh = z[:, GDN_DV * h:GDN_DV * (h + 1)]
        gated.append(_rms(o, gw_ref[...]) * (zh * jax.nn.sigmoid(zh)))
    mixed = jnp.concatenate([oa_ref[0], jnp.concatenate(gated, axis=1).astype(BF16), oc_ref[0]], axis=1)
    y = jnp.dot(mixed, wo_ref[...], preferred_element_type=F32)
    x1 = x_ref[0] + mod_ref[0, 2:3, :] * _rms(y, post_ref[...])
    x1_ref[0] = x1
    h2 = (_rms(x1, pre2_ref[...]) * (1.0 + mod_ref[0, 4:5, :]) + mod_ref[0, 3:4, :]).astype(BF16)
    h2_ref[...] = h2

    scores = jax.nn.sigmoid(_nt_dot(rw_ref[...], h2))
    sel = scores + rb_ref[...]
    gsz = N_EXPERTS // MOE_GROUPS
    g3 = sel.reshape(MOE_GROUPS, gsz, TM)
    io = lax.broadcasted_iota(jnp.int32, g3.shape, 1)
    m1 = jnp.max(g3, axis=1, keepdims=True)
    i1 = jnp.min(jnp.where(g3 == m1, io, gsz), axis=1, keepdims=True)
    m2 = jnp.max(jnp.where(io == i1, -jnp.inf, g3), axis=1, keepdims=True)

    def top_mask(vals, k):
        n = vals.shape[0]
        idx = lax.broadcasted_iota(jnp.int32, vals.shape, 0)
        mask = jnp.zeros(vals.shape, F32)
        for _ in range(k):
            mx = jnp.max(vals, axis=0, keepdims=True)
            hit = idx == jnp.min(jnp.where(vals == mx, idx, n), axis=0, keepdims=True)
            mask = jnp.where(hit, 1.0, mask)
            vals = jnp.where(hit, -jnp.inf, vals)
        return mask

    gmask = top_mask((m1 + m2).reshape(MOE_GROUPS, TM), MOE_TOPK_GROUPS)
    masked = jnp.where(gmask.reshape(MOE_GROUPS, 1, TM) > 0.0, g3, -jnp.inf).reshape(N_EXPERTS, TM)
    smask = top_mask(masked, MOE_TOP_K)
    w = jnp.where(smask > 0.0, scores, 0.0)
    wn = w / jnp.sum(w, axis=0, keepdims=True) * ROUTED_SCALE
    wd_ref[...] = jnp.where(smask > 0.0, wn, -1.0)

    ci = lax.broadcasted_iota(jnp.int32, (TM, TM), 0)
    cj = lax.broadcasted_iota(jnp.int32, (TM, TM), 1)
    ut = jnp.where(ci <= cj, 1.0, 0.0).astype(BF16)
    cs = jnp.dot(smask.astype(BF16), ut, preferred_element_type=F32)
    carry = carry_ref[...]
    posf = jnp.where(smask > 0.0, carry + cs - 1.0, -1.0)
    pos_ref[...] = posf.astype(jnp.int32)
    rankt_ref[...] = posf.T
    carry = carry + jnp.broadcast_to(cs[:, TM - 1:TM], (N_EXPERTS, TM))
    carry_ref[...] = carry
    cum_ref[0] = carry[:, :LANE]


def _mix_out(xx, modl, oa, o_f, o_b, z, oc, gw, wo, post, pre2, rw, rb, n_lat, nt, tiles_per_super):
    B, T, D = xx.shape
    nb = modl.shape[0] - 1
    N = B * nt * TM

    def mod_map(b, t):
        return (jnp.where(t >= n_lat, nb, b), 0, 0)

    row = lambda w: pl.BlockSpec((1, TM, w), lambda b, t: (b, t, 0))
    flat = lambda b, t: (0, b * nt + t)
    kern = functools.partial(_mix_out_kernel, tiles_per_super=tiles_per_super)
    return pl.pallas_call(
        kern,
        grid=(B, nt),
        in_specs=[row(D), pl.BlockSpec((1, 6, D), mod_map), row(384),
                  pl.BlockSpec((1, GDN_HEADS, TM, GDN_DV), lambda b, t: (b, 0, t, 0)),
                  pl.BlockSpec((1, GDN_HEADS, TM, GDN_DV), lambda b, t: (b, 0, t, 0)),
                  row(384), row(256), _const_spec((1, GDN_DV)), _const_spec(wo.shape),
                  _const_spec((1, D)), _const_spec((1, D)), _const_spec(rw.shape), _const_spec((N_EXPERTS, 1))],
        out_specs=[row(D),
                   pl.BlockSpec((TM, D), lambda b, t: (b * nt + t, 0)),
                   pl.BlockSpec((N_EXPERTS, TM), flat),
                   pl.BlockSpec((N_EXPERTS, TM), flat),
                   pl.BlockSpec((TM, N_EXPERTS), lambda b, t: (b * nt + t, 0)),
                   pl.BlockSpec((1, N_EXPERTS, LANE), lambda b, t: (b * nt + t, 0, 0))],
        out_shape=[jax.ShapeDtypeStruct((B, nt * TM, D), F32),
                   jax.ShapeDtypeStruct((N, D), BF16),
                   jax.ShapeDtypeStruct((N_EXPERTS, N), F32),
                   jax.ShapeDtypeStruct((N_EXPERTS, N), jnp.int32),
                   jax.ShapeDtypeStruct((N, N_EXPERTS), F32),
                   jax.ShapeDtypeStruct((N // TM, N_EXPERTS, LANE), F32)],
        scratch_shapes=[pltpu.VMEM((N_EXPERTS, TM), F32)],
        compiler_params=_cparams(("arbitrary", "arbitrary")),
        name="mix_out_router",
    )(xx, modl, oa, o_f, o_b, z, oc, gw, wo, post, pre2, rw, rb)


MOE_BLOCK = 128


MOE_WINDOW_TILES = 6
N_PAIRS = N_EXPERTS // 2


def _moe_meta(cum, N, ST):
    tps, ns = ST // TM, N // ST
    kmax = ST // MOE_BLOCK
    cumt = cum[:, :, 0].astype(jnp.int32).reshape(ns, tps, N_EXPERTS)
    cnt = cumt[:, -1, :]
    nblk = (cnt + MOE_BLOCK - 1) // MOE_BLOCK
    k0 = jnp.arange(kmax, dtype=jnp.int32) * MOE_BLOCK
    kend = jnp.minimum(k0[None, None, :] + MOE_BLOCK, cnt[:, :, None])
    nonempty = k0[None, None, :] < cnt[:, :, None]
    lo = jnp.sum(cumt[:, :, :, None] <= k0[None, None, None, :], axis=1)
    hi = jnp.sum(cumt[:, :, :, None] < kend[:, None, :, :], axis=1)
    lo = jnp.where(nonempty, lo, tps)
    hi = jnp.where(nonempty, hi, -1)
    lo = jnp.minimum(lo[:, 0::2], lo[:, 1::2]).reshape(-1).astype(jnp.int32)
    hi = jnp.maximum(hi[:, 0::2], hi[:, 1::2]).reshape(-1).astype(jnp.int32)
    nslot = jnp.maximum(nblk[:, 0::2], nblk[:, 1::2]).reshape(-1).astype(jnp.int32)
    return nslot, lo, hi


def _moe_kernel(nslot_ref, lo_ref, hi_ref, h2_ref, pos_ref, post_ref, wd_ref, wg_ref, wu_ref, wdn_ref,
                out_ref, xg_ref, wc_ref, *, tps, kmax):
    g = pl.program_id(0)
    pair = g % N_PAIRS
    nw = min(MOE_WINDOW_TILES, tps)
    wn = nw * TM
    R = 2 * MOE_BLOCK

    @pl.when(pair == 0)
    def _zero():
        out_ref[...] = jnp.zeros_like(out_ref)

    def ffn(xe, i):
        hg = jnp.dot(xe, wg_ref[i], preferred_element_type=F32)
        hu = jnp.dot(xe, wu_ref[i], preferred_element_type=F32)
        return jnp.dot((hg * jax.nn.sigmoid(hg) * hu).astype(BF16), wdn_ref[i], preferred_element_type=F32)

    def slot(k, carry):
        lo, hi = lo_ref[g * kmax + k], hi_ref[g * kmax + k]
        k0 = k * MOE_BLOCK
        nwin = (hi - lo + nw) // nw
        rr = lax.broadcasted_iota(jnp.int32, (R, wn), 0)
        second = rr >= MOE_BLOCK
        rank = k0 + jnp.bitwise_and(rr, MOE_BLOCK - 1)
        xg_ref[...] = jnp.zeros_like(xg_ref)
        wc_ref[...] = jnp.zeros_like(wc_ref)

        def window(w):
            w0 = lo + w * nw
            start = jnp.minimum(w0, tps - nw)
            return w0, start, pl.multiple_of(start * TM, TM)

        def gather(w, c):
            w0, start, off = window(w)
            tok = off + lax.broadcasted_iota(jnp.int32, (R, wn), 1)
            prow = jnp.where(second, pos_ref[1, :, pl.ds(off, wn)], pos_ref[0, :, pl.ds(off, wn)])
            hit = jnp.where(tok >= w0 * TM, prow, -2) == rank
            xg_ref[...] += jnp.dot(jnp.where(hit, 1.0, 0.0).astype(BF16), h2_ref[pl.ds(off, wn), :],
                                   preferred_element_type=F32)
            wrow = jnp.where(second, wd_ref[1, :, pl.ds(off, wn)], wd_ref[0, :, pl.ds(off, wn)])
            wc_ref[...] += jnp.sum(jnp.where(hit, wrow, 0.0), axis=1, keepdims=True)
            return c

        lax.fori_loop(0, nwin, gather, 0)
        xb = xg_ref[...].astype(BF16)
        y = jnp.concatenate([ffn(xb[:MOE_BLOCK], 0), ffn(xb[MOE_BLOCK:], 1)], axis=0)
        yw = (y * wc_ref[:, 0:1]).astype(BF16)

        li = lax.broadcasted_iota(jnp.int32, (TM, R), 1)
        second_t = li >= MOE_BLOCK
        rank_t = (k0 + jnp.bitwise_and(li, MOE_BLOCK - 1)).astype(F32)
        lane_e = lax.broadcasted_iota(jnp.int32, (TM, N_EXPERTS), 1)

        def scatter(w, c):
            w0, start, off = window(w)
            for j in range(nw):
                tile = start + j

                @pl.when(jnp.logical_and(tile >= w0, tile <= hi))
                def _tile():
                    toff = pl.multiple_of(tile * TM, TM)
                    pt = post_ref[pl.ds(toff, TM), :]
                    ca = jnp.sum(jnp.where(lane_e == 2 * pair, pt, 0.0), axis=1, keepdims=True)
                    cb = jnp.sum(jnp.where(lane_e == 2 * pair + 1, pt, 0.0), axis=1, keepdims=True)
                    hit_t = jnp.where(second_t, cb, ca) == rank_t
                    out_ref[pl.ds(toff, TM), :] += jnp.dot(jnp.where(hit_t, 1.0, 0.0).astype(BF16), yw,
                                                           preferred_element_type=F32)
            return c

        lax.fori_loop(0, nwin, scatter, 0)
        return carry

    lax.fori_loop(0, nslot_ref[g], slot, 0)


def _moe_routed(h2, wd, pos, post, meta, wg, wu, wdn, ST):
    N, D = h2.shape
    E = N_EXPERTS
    tps, kmax = ST // TM, ST // MOE_BLOCK
    sup = lambda g, *_: g // N_PAIRS
    par = lambda g, *_: g % N_PAIRS
    grid_spec = pltpu.PrefetchScalarGridSpec(
        num_scalar_prefetch=3,
        grid=((N // ST) * N_PAIRS,),
        in_specs=[pl.BlockSpec((ST, D), lambda g, *_: (sup(g), 0)),
                  pl.BlockSpec((2, 1, ST), lambda g, *_: (par(g), 0, sup(g))),
                  pl.BlockSpec((ST, E), lambda g, *_: (sup(g), 0)),
                  pl.BlockSpec((2, 1, ST), lambda g, *_: (par(g), 0, sup(g))),
                  pl.BlockSpec((2, D, D_EXPERT), lambda g, *_: (par(g), 0, 0)),
                  pl.BlockSpec((2, D, D_EXPERT), lambda g, *_: (par(g), 0, 0)),
                  pl.BlockSpec((2, D_EXPERT, D), lambda g, *_: (par(g), 0, 0))],
        out_specs=pl.BlockSpec((ST, D), lambda g, *_: (sup(g), 0)),
        scratch_shapes=[pltpu.VMEM((2 * MOE_BLOCK, D), F32), pltpu.VMEM((2 * MOE_BLOCK, LANE), F32)],
    )
    return pl.pallas_call(
        functools.partial(_moe_kernel, tps=tps, kmax=kmax),
        grid_spec=grid_spec,
        out_shape=jax.ShapeDtypeStruct((N, D), F32),
        compiler_params=_cparams(("arbitrary",)),
        name="moe_routed",
    )(*meta, h2, pos.reshape(E, 1, N), post, wd.reshape(E, 1, N), wg, wu, wdn)


def _ffn_out_kernel(x1_ref, mod_ref, h2_ref, routed_ref, sg_ref, su_ref, sd_ref, post_ref, o_ref):
    h2 = h2_ref[...]
    hg = jnp.dot(h2, sg_ref[...], preferred_element_type=F32)
    hu = jnp.dot(h2, su_ref[...], preferred_element_type=F32)
    f = jnp.dot((hg * jax.nn.sigmoid(hg) * hu).astype(BF16), sd_ref[...], preferred_element_type=F32) + routed_ref[...]
    o_ref[0] = x1_ref[0] + mod_ref[0, 5:6, :] * _rms(f, post_ref[...])


def _ffn_out(x1, modl, h2, routed, sg, su, sd, post, n_lat):
    B, Tn, D = x1.shape
    nt = Tn // TM
    nb = modl.shape[0] - 1

    def mod_map(b, t):
        return (jnp.where(t >= n_lat, nb, b), 0, 0)

    row = pl.BlockSpec((1, TM, D), lambda b, t: (b, t, 0))
    flat = pl.BlockSpec((TM, D), lambda b, t: (b * nt + t, 0))
    return pl.pallas_call(
        _ffn_out_kernel,
        grid=(B, nt),
        in_specs=[row, pl.BlockSpec((1, 6, D), mod_map), flat, flat,
                  _const_spec(sg.shape), _const_spec(su.shape), _const_spec(sd.shape), _const_spec((1, D))],
        out_specs=row,
        out_shape=jax.ShapeDtypeStruct((B, Tn, D), F32),
        compiler_params=_cparams(("parallel", "parallel")),
        name="ffn_out",
    )(x1, modl, h2, routed, sg, su, sd, post)


def _tiles_per_super(n_tiles):
    return max(k for k in range(1, 13) if n_tiles % k == 0)


def kernel(x, c, ctx, c_ctx, ada_w, ada_b, mix_norm_pre, mix_norm_post, ffn_norm_pre, ffn_norm_post, w_in, w_out, swa_sink, gdn_conv_w, gdn_a_log, gdn_dt_bias, gdn_norm_w, mla_q_norm, mla_w_uq, mla_kv_norm, mla_w_ukv, router_w, router_bias, expert_w_gate, expert_w_up, expert_w_down, shared_w_gate, shared_w_up, shared_w_down):
    B, S, D = x.shape
    L = ctx.shape[1]
    assert L == TM and S % (2 * TM) == 0
    T = S + L
    n_lat = S // TM
    xx = jnp.concatenate([x, ctx], axis=1)
    tabs = _rope_tables(S, T)
    cvecs = jnp.concatenate([c, c_ctx[None], jnp.zeros((8 - B - 1, D), F32)], axis=0)
    mods = _modulation(cvecs, ada_w, ada_b).reshape(DEPTH, 8, 6, D)[:, :B + 1]
    for layer in range(DEPTH):
        xx = _layer(layer, xx, mods[layer], tabs, S, layer == DEPTH - 1,
                    mix_norm_pre, mix_norm_post, ffn_norm_pre, ffn_norm_post, w_in, w_out, swa_sink, gdn_conv_w,
                    gdn_a_log, gdn_dt_bias, gdn_norm_w, mla_q_norm, mla_w_uq, mla_kv_norm, mla_w_ukv, router_w,
                    router_bias, expert_w_gate, expert_w_up, expert_w_down, shared_w_gate, shared_w_up, shared_w_down)
    return xx


def _layer(layer, xx, modl, tabs, S, last, mix_norm_pre, mix_norm_post, ffn_norm_pre, ffn_norm_post, w_in, w_out,
           swa_sink, gdn_conv_w, gdn_a_log, gdn_dt_bias, gdn_norm_w, mla_q_norm, mla_w_uq, mla_kv_norm, mla_w_ukv,
           router_w, router_bias, expert_w_gate, expert_w_up, expert_w_down, shared_w_gate, shared_w_up, shared_w_down):
    B, T, D = xx.shape
    n_lat = S // TM
    nt = n_lat if last else T // TM
    qa, ka, va, gx, z, ba, qm, km, vm = _project(
        xx, modl, mix_norm_pre[layer][None], _prep_w_in(w_in[layer]), _prep_w_uq(mla_w_uq[layer]),
        _prep_w_ukv(mla_w_ukv[layer]), jnp.pad(mla_q_norm[layer], (0, 256 - MLA_Q_RANK))[None],
        mla_kv_norm[layer][None], tabs, n_lat)
    oa = _swa_attention(swa_sink[layer], qa, ka, va, S, nt)
    oc = _mla_attention(qm, km, vm, S, nt)
    gp = jnp.zeros((2, LANE), F32)
    g0 = _BA_LANE + 2 * GDN_HEADS
    gp = gp.at[0, g0:g0 + 2 * GDN_HEADS].set(gdn_a_log[layer].reshape(-1))
    gp = gp.at[1, g0:g0 + 2 * GDN_HEADS].set(gdn_dt_bias[layer].reshape(-1))
    p1, p2, p3, gt = _gdn_local(gx, gdn_conv_w[layer], ba, gp, n_lat)
    o_f, o_b = _gdn_scan(p1, p2, p3, gt, S)

    tps = _tiles_per_super(B * nt)
    ST = tps * TM
    N = B * nt * TM
    x1, h2, wd, pos, post, cum = _mix_out(
        xx, modl, oa, o_f, o_b, z, oc, gdn_norm_w[layer][None], _prep_w_out(w_out[layer]),
        mix_norm_post[layer][None], ffn_norm_pre[layer][None], router_w[layer].T.astype(BF16),
        router_bias[layer][:, None], n_lat, nt, tps)
    routed = _moe_routed(h2, wd, pos, post, _moe_meta(cum, N, ST), expert_w_gate[layer].astype(BF16),
                         expert_w_up[layer].astype(BF16), expert_w_down[layer].astype(BF16), ST)
    return _ffn_out(x1, modl, h2, routed, shared_w_gate[layer].astype(BF16), shared_w_up[layer].astype(BF16),
                    shared_w_down[layer].astype(BF16), ffn_norm_post[layer][None], n_lat)
```

```python
import functools
import math

import numpy as np
import jax
import jax.numpy as jnp
from jax import lax
from jax.experimental import pallas as pl
from jax.experimental.pallas import tpu as pltpu

F32 = jnp.float32
BF16 = jnp.bfloat16

DEPTH = 2
GRID_W = 64
EPS = 1e-6
ROPE_BASE = 10000.0
HEAD_DIM = 64
SWA_HEADS = 6
SWA_KV_HEADS = 2
SWA_WINDOW = 128
GDN_HEADS = 6
GDN_DK = 64
GDN_DV = 64
GDN_CONV = 5
GDN_CHUNK = 64
MLA_HEADS = 4
MLA_Q_RANK = 192
MLA_KV_RANK = 128
MLA_NOPE = 64
MLA_ROPE = 32
MLA_V = 64
N_EXPERTS = 64
MOE_TOP_K = 8
MOE_GROUPS = 8
MOE_TOPK_GROUPS = 4
D_EXPERT = 256
ROUTED_SCALE = 2.5
LOG2_E = math.log2(math.e)

_SPLITS = (384, 128, 128, 1152, 384, 24, 192, 128, 32)
_OFF = np.concatenate([[0], np.cumsum(_SPLITS)]).tolist()
D_PROJ = _OFF[-1]

TM = 256
LANE = 128
VMEM_LIMIT = 56 * 1024 * 1024

_C_QA, _C_QAS, _C_KA, _C_KAS, _C_VA, _C_GDN, _C_Z, _C_CQ, _C_CKV, _C_KRB, _C_END = (
    0, 768, 1536, 1664, 1792, 1920, 3072, 3456, 3712, 3840, 3968)
_BA_LANE = 32


def _cparams(sem, vmem=VMEM_LIMIT):
    return pltpu.CompilerParams(dimension_semantics=sem, vmem_limit_bytes=vmem)


def _rope_partner(d, width):
    half, n = width // 2, width // 4
    i = d % half
    return (d // half) * half + (i + n if i < n else i - n)


def _take_cols(w, idx):
    idx = [int(i) for i in idx]
    pieces, start = [], 0
    for pos in range(1, len(idx) + 1):
        run_ends = pos == len(idx) or (idx[pos] != idx[pos - 1] + 1 if idx[pos - 1] >= 0 else idx[pos] >= 0) \
            or (idx[pos] < 0) != (idx[pos - 1] < 0)
        if run_ends:
            n = pos - start
            pieces.append(jnp.zeros((w.shape[0], n), w.dtype) if idx[start] < 0 else w[:, idx[start]:idx[start] + n])
            start = pos
    return jnp.concatenate(pieces, axis=1)


def _prep_w_in(w_in):
    cols = []
    for swap in (False, True):
        for h in range(SWA_HEADS):
            j = h // (SWA_HEADS // SWA_KV_HEADS)
            blk = [-1] * LANE
            for d in range(HEAD_DIM):
                blk[64 * j + d] = _OFF[0] + h * HEAD_DIM + (_rope_partner(d, HEAD_DIM) if swap else d)
            cols += blk
    for swap in (False, True):
        for j in range(SWA_KV_HEADS):
            cols += [_OFF[1] + j * HEAD_DIM + (_rope_partner(d, HEAD_DIM) if swap else d) for d in range(HEAD_DIM)]
    cols += list(range(_OFF[2], _OFF[3]))
    cols += list(range(_OFF[3], _OFF[4]))
    cols += list(range(_OFF[4], _OFF[5]))
    cols += list(range(_OFF[6], _OFF[7])) + [-1] * 64
    cols += list(range(_OFF[7], _OFF[8]))
    cols += list(range(_OFF[8], _OFF[9])) + list(range(_OFF[5], _OFF[6])) + [-1] * (LANE - 32 - 24)
    assert len(cols) == _C_END
    return _take_cols(w_in, cols).astype(BF16)


def _prep_w_uq(w_uq):
    cols = []
    for swap in (False, True):
        for h in range(MLA_HEADS):
            base = h * (MLA_NOPE + MLA_ROPE)
            blk = [-1] * LANE
            for d in range(MLA_NOPE):
                blk[d] = -1 if swap else base + d
            for r in range(MLA_ROPE):
                blk[MLA_NOPE + r] = base + MLA_NOPE + (_rope_partner(r, MLA_ROPE) if swap else r)
            cols += blk
    w = _take_cols(w_uq, cols)
    return jnp.pad(w, ((0, 256 - MLA_Q_RANK), (0, 0))).astype(BF16)


def _prep_w_ukv(w_ukv):
    kcols, vcols = [], []
    for h in range(MLA_HEADS):
        base = h * (MLA_NOPE + MLA_V)
        kcols += [base + d for d in range(MLA_NOPE)] + [-1] * 64
        vcols += [base + MLA_NOPE + d for d in range(MLA_V)]
    wk = _take_cols(w_ukv, kcols)
    wv = _take_cols(w_ukv, vcols)
    top = jnp.concatenate([wk, jnp.zeros_like(wk), wv], axis=1)
    place = np.zeros((128, 1280), np.float32)
    for h in range(MLA_HEADS):
        for r in range(MLA_ROPE):
            place[r, 128 * h + MLA_NOPE + r] = 1.0
            place[_rope_partner(r, MLA_ROPE), 512 + 128 * h + MLA_NOPE + r] = 1.0
    return jnp.concatenate([top, jnp.asarray(place)], axis=0).astype(BF16)


def _prep_w_out(w_out):
    rows = []
    G = SWA_HEADS // SWA_KV_HEADS
    for g in range(G):
        for j in range(SWA_KV_HEADS):
            rows += [(G * j + g) * HEAD_DIM + d for d in range(HEAD_DIM)]
    rows += list(range(SWA_HEADS * HEAD_DIM, w_out.shape[0]))
    return jnp.take(w_out, jnp.asarray(rows), axis=0).astype(BF16)


def _rope_tables(S, T):
    t = np.arange(S)
    row, col = t // GRID_W, t % GRID_W

    def tab(width, lanes):
        half, n = width // 2, width // 4
        c = np.ones((T, LANE), np.float64)
        s = np.zeros((T, LANE), np.float64)
        for lane, d in lanes:
            i = d % half
            pos = row if d < half else col
            ang =(pos.astype(np.float32) * np.float32(ROPE_BASE ** (-(i % n) / n))).astype(np.float64)
            c[:S, lane] = np.cos(ang)
            s[:S, lane] = -np.sin(ang) if i < n else np.sin(ang)
        return jnp.asarray(c, F32), jnp.asarray(s, F32)

    ca, sa = tab(HEAD_DIM, [(l, l % HEAD_DIM) for l in range(LANE)])
    cm, sm = tab(MLA_ROPE, [(MLA_NOPE + r, r) for r in range(MLA_ROPE)])
    return ca, sa, cm, sm


def _mod_kernel(c_ref, w_ref, b_ref, o_ref):
    cv = c_ref[...]
    a = (cv * jax.nn.sigmoid(cv)).astype(BF16)
    o_ref[0] = jnp.dot(a, w_ref[0].astype(BF16), preferred_element_type=F32) + b_ref[0]


def _modulation(cvecs, ada_w, ada_b):
    depth, D, N = ada_w.shape
    tn = 512
    return pl.pallas_call(
        _mod_kernel,
        grid=(depth, N // tn),
        in_specs=[pl.BlockSpec((8, D), lambda l, j: (0, 0)),
                  pl.BlockSpec((1, D, tn), lambda l, j: (l, 0, j)),
                  pl.BlockSpec((1, 1, tn), lambda l, j: (l, 0, j))],
        out_specs=pl.BlockSpec((1, 8, tn), lambda l, j: (l, 0, j)),
        out_shape=jax.ShapeDtypeStruct((depth, 8, N), F32),
        compiler_params=_cparams(("arbitrary", "arbitrary")),
        name="modulation",
    )(cvecs, ada_w, ada_b.reshape(depth, 1, N))


def _proj_kernel(x_ref, mod_ref, gain_ref, w_ref, wq2_ref, wk2_ref, qg_ref, kvg_ref,
                 ca_ref, sa_ref, cm_ref, sm_ref,
                 qa_ref, ka_ref, va_ref, gx_ref, z_ref, ba_ref, qm_ref, km_ref, vm_ref):
    x = x_ref[0]
    ms = jnp.mean(x * x, axis=-1, keepdims=True)
    h = x * lax.rsqrt(ms + EPS) * gain_ref[...]
    h = h * (1.0 + mod_ref[0, 1:2, :]) + mod_ref[0, 0:1, :]
    hb = h.astype(BF16)

    ca, sa, cm, sm = ca_ref[...], sa_ref[...], cm_ref[...], sm_ref[...]

    def rope(a, b, c, s):
        n = a.shape[1] // LANE
        return a * jnp.concatenate([c] * n, axis=1) + b * jnp.concatenate([s] * n, axis=1)

    p1 = jnp.dot(hb, w_ref[:, _C_QA:_C_GDN], preferred_element_type=F32)
    qa = rope(p1[:, _C_QA:_C_QAS], p1[:, _C_QAS:_C_KA], ca, sa)
    qa_ref[0] = (qa * (HEAD_DIM ** -0.5)).astype(BF16)
    ka_ref[0] = rope(p1[:, _C_KA:_C_KAS], p1[:, _C_KAS:_C_VA], ca, sa).astype(BF16)
    va_ref[0] = p1[:, _C_VA:_C_GDN].astype(BF16)

    p2 = jnp.dot(hb, w_ref[:, _C_GDN:_C_Z], preferred_element_type=F32)
    for j in range(3 * GDN_HEADS):
        gx_ref[0, j] = p2[:, 64 * j:64 * j + 64]

    p3 = jnp.dot(hb, w_ref[:, _C_Z:_C_END], preferred_element_type=F32)
    z_ref[0] = p3[:, 0:_C_CQ - _C_Z]
    cq = p3[:, _C_CQ - _C_Z:_C_CKV - _C_Z]
    ckv = p3[:, _C_CKV - _C_Z:_C_KRB - _C_Z]
    krb = p3[:, _C_KRB - _C_Z:]
    ba_ref[0] = krb

    cqn = cq * lax.rsqrt(jnp.sum(cq * cq, axis=-1, keepdims=True) * (1.0 / MLA_Q_RANK) + EPS) * qg_ref[...]
    e = jnp.dot(cqn.astype(BF16), wq2_ref[...], preferred_element_type=F32)
    qm = rope(e[:, :512], e[:, 512:], cm, sm)
    qm_ref[0] = (qm * (LOG2_E * (MLA_NOPE + MLA_ROPE) ** -0.5)).astype(BF16)

    ckvn = ckv * lax.rsqrt(jnp.mean(ckv * ckv, axis=-1, keepdims=True) + EPS) * kvg_ref[...]
    lhs2 = jnp.concatenate([ckvn.astype(BF16), krb.astype(BF16)], axis=1)
    e2 = jnp.dot(lhs2, wk2_ref[...], preferred_element_type=F32)
    km_ref[0] = rope(e2[:, :512], e2[:, 512:1024], cm, sm).astype(BF16)
    vm_ref[0] = e2[:, 1024:].astype(BF16)


def _const_spec(shape):
    nd = len(shape)
    return pl.BlockSpec(shape, lambda *_: (0,) * nd)


def _project(xx, modl, gain, w_main, wq2, wk2, qg, kvg, tabs, n_lat_tiles):
    B, T, D = xx.shape
    nt = T // TM
    nb = modl.shape[0] - 1

    def mod_map(b, t):
        return (jnp.where(t >= n_lat_tiles, nb, b), 0, 0)

    row = lambda w: pl.BlockSpec((1, TM, w), lambda b, t: (b, t, 0))
    tab = pl.BlockSpec((TM, LANE), lambda b, t: (t, 0))
    out_shapes = [
        jax.ShapeDtypeStruct((B, T, 768), BF16),
        jax.ShapeDtypeStruct((B, T, 128), BF16),
        jax.ShapeDtypeStruct((B, T, 128), BF16),
        jax.ShapeDtypeStruct((B, 18, T, 64), F32),
        jax.ShapeDtypeStruct((B, T, 384), F32),
        jax.ShapeDtypeStruct((B, T, 128), F32),
        jax.ShapeDtypeStruct((B, T, 512), BF16),
        jax.ShapeDtypeStruct((B, T, 512), BF16),
        jax.ShapeDtypeStruct((B, T, 256), BF16),
    ]
    out_specs = [row(768), row(128), row(128),
                 pl.BlockSpec((1, 18, TM, 64), lambda b, t: (b, 0, t, 0)),
                 row(384), row(128), row(512), row(512), row(256)]
    return pl.pallas_call(
        _proj_kernel,
        grid=(B, nt),
        in_specs=[row(D), pl.BlockSpec((1, 6, D), mod_map), _const_spec((1, D)),
                  _const_spec(w_main.shape), _const_spec(wq2.shape), _const_spec(wk2.shape),
                  _const_spec((1, 256)), _const_spec((1, 128)), tab, tab, tab, tab],
        out_specs=out_specs,
        out_shape=out_shapes,
        compiler_params=_cparams(("parallel", "parallel")),
        name="in_proj",
    )(xx, modl, gain, w_main, wq2, wk2, qg, kvg, *tabs)


def _nt_dot(a, b):
    return lax.dot_general(a, b, (((1,), (1,)), ((), ())), preferred_element_type=F32)


def _swa_kernel(sink_ref, q_ref, k_ref, v_ref, o_ref, *, S, n_lat):
    i = pl.program_id(1)
    G = SWA_HEADS // SWA_KV_HEADS
    W = 2 * TM
    lane = lax.broadcasted_iota(jnp.int32, (TM, LANE), 1)
    kc = k_ref[0, pl.ds(S, TM), :]
    vc = v_ref[0, pl.ds(S, TM), :]

    def heads(local):
        outs = []
        for g in range(G):
            og = []
            for j in range(SWA_KV_HEADS):
                h = G * j + g
                q = q_ref[0, :, LANE * h:LANE * (h + 1)]
                sink = sink_ref[h]
                s_ctx = _nt_dot(q, kc)
                m = jnp.maximum(jnp.max(s_ctx, axis=-1, keepdims=True), sink)
                if local is not None:
                    kw, vw, valid = local
                    s_loc = jnp.where(valid, _nt_dot(q, kw), -jnp.inf)
                    m = jnp.maximum(m, jnp.max(s_loc, axis=-1, keepdims=True))
                p_ctx = jnp.exp(s_ctx - m)
                den = jnp.sum(p_ctx, axis=-1, keepdims=True) + jnp.exp(sink - m)
                o = jnp.dot(p_ctx.astype(BF16), vc, preferred_element_type=F32)
                if local is not None:
                    p_loc = jnp.exp(s_loc - m)
                    den = den + jnp.sum(p_loc, axis=-1, keepdims=True)
                    o = o + jnp.dot(p_loc.astype(BF16), vw, preferred_element_type=F32)
                og.append(o * (1.0 / den))
            outs.append(jnp.where(lane < HEAD_DIM, og[0], og[1]))
        o_ref[0] = jnp.concatenate(outs, axis=1).astype(BF16)

    @pl.when(i < n_lat)
    def _latent():
        start = pl.multiple_of(jnp.clip(i * TM - SWA_WINDOW, 0, S - W), LANE)
        kw = k_ref[0, pl.ds(start, W), :]
        vw = v_ref[0, pl.ds(start, W), :]
        qpos = i * TM + lax.broadcasted_iota(jnp.int32, (TM, W), 0)
        kpos = start + lax.broadcasted_iota(jnp.int32, (TM, W), 1)
        heads((kw, vw, jnp.abs(qpos - kpos) <= SWA_WINDOW))

    @pl.when(i >= n_lat)
    def _context():
        heads(None)


def _swa_attention(sink, qa, ka, va, S, nt):
    B, T, _ = qa.shape
    kern = functools.partial(_swa_kernel, S=S, n_lat=S // TM)
    return pl.pallas_call(
        kern,
        grid=(B, nt),
        in_specs=[pl.BlockSpec(memory_space=pltpu.SMEM),
                  pl.BlockSpec((1, TM, 768), lambda b, t: (b, t, 0)),
                  pl.BlockSpec((1, T, 128), lambda b, t: (b, 0, 0)),
                  pl.BlockSpec((1, T, 128), lambda b, t: (b, 0, 0))],
        out_specs=pl.BlockSpec((1, TM, 384), lambda b, t: (b, t, 0)),
        out_shape=jax.ShapeDtypeStruct((B, nt * TM, 384), BF16),
        compiler_params=_cparams(("parallel", "parallel")),
        name="swa_attention",
    )(sink, qa, ka, va)


MLA_SUBSTEPS = 2


def _mla_kernel(q_ref, k_ref, v_ref, o_ref, *, S, n_lat):
    i = pl.program_id(1)
    lane = lax.broadcasted_iota(jnp.int32, (TM, LANE), 1)
    MLA_KV_CHUNK = max(c for c in (4096, 2048, 1024, 512) if S % c == 0)
    n_chunks = jnp.where(i < n_lat, S // MLA_KV_CHUNK, 0)

    def lane_blocks(s):
        return [s[:, LANE * j:LANE * (j + 1)] for j in range(s.shape[1] // LANE)]

    outs = []
    for h in range(MLA_HEADS):
        q = q_ref[0, :, LANE * h:LANE * (h + 1)]

        def scores(off, size):
            return _nt_dot(q, k_ref[0, pl.ds(off, size), LANE * h:LANE * (h + 1)])

        def values(off, size):
            return v_ref[0, pl.ds(off, size), LANE * (h // 2):LANE * (h // 2 + 1)]

        def step(off, size, carry):
            m, l, acc = carry
            blocks = lane_blocks(scores(off, size))
            mx = blocks[0]
            for blk in blocks[1:]:
                mx = jnp.maximum(mx, blk)
            m_new = jnp.maximum(m, jnp.broadcast_to(jnp.max(mx, axis=-1, keepdims=True), (TM, LANE)))
            alpha = jnp.exp2(m - m_new)
            ps = [jnp.exp2(blk - m_new) for blk in blocks]
            l = alpha * l
            for p in ps:
                l = l + p
            p = jnp.concatenate(ps, axis=1).astype(BF16)
            return m_new, l, alpha * acc + jnp.dot(p, values(off, size), preferred_element_type=F32)

        def body(c, cr):
            off = pl.multiple_of(c * MLA_KV_CHUNK, MLA_KV_CHUNK)
            for u in range(MLA_SUBSTEPS):
                cr = step(pl.multiple_of(off + u * sub, TM), sub, cr)
            return cr

        sub = MLA_KV_CHUNK // MLA_SUBSTEPS
        zero = jnp.zeros((TM, LANE), F32)
        carry = lax.fori_loop(0, n_chunks, body, (jnp.full((TM, LANE), -jnp.inf, F32), zero, zero))
        m, l, acc = step(S, TM, carry)
        outs.append(acc * (1.0 / jnp.sum(l, axis=-1, keepdims=True)))
    o_ref[0] = jnp.concatenate([jnp.where(lane < MLA_V, outs[0], outs[1]),
                                jnp.where(lane < MLA_V, outs[2], outs[3])], axis=1).astype(BF16)


def _mla_attention(qm, km, vm, S, nt):
    B, T, _ = qm.shape
    kern = functools.partial(_mla_kernel, S=S, n_lat=S // TM)
    return pl.pallas_call(
        kern,
        grid=(B, nt),
        in_specs=[pl.BlockSpec((1, TM, 512), lambda b, t: (b, t, 0)),
                  pl.BlockSpec((1, T, 512), lambda b, t: (b, 0, 0)),
                  pl.BlockSpec((1, T, 256), lambda b, t: (b, 0, 0))],
        out_specs=pl.BlockSpec((1, TM, 256), lambda b, t: (b, t, 0)),
        out_shape=jax.ShapeDtypeStruct((B, nt * TM, 256), BF16),
        compiler_params=_cparams(("parallel", "parallel")),
        name="mla_attention",
    )(qm, km, vm)


CH = GDN_CHUNK
NCH = TM // CH


def _split3(a):
    hi = a.astype(BF16)
    r = a - hi.astype(F32)
    mid = r.astype(BF16)
    return hi, mid, (r - mid.astype(F32)).astype(BF16)


def _bmm3(a, b):
    ah, al, _ = _split3(a)
    bh, bl, _ = _split3(b)
    lhs = jnp.concatenate([ah, al, ah], axis=-1)
    rhs = jnp.concatenate([bh, bh, bl], axis=1)
    return jnp.einsum('cij,cjk->cik', lhs, rhs, preferred_element_type=F32)


GDN_HEADS_PER_STEP = 2


def _gdn_local_kernel(x_ref, xp_ref, xn_ref, cw_ref, ba_ref, gp_ref,
                      p1_ref, p2_ref, p3_ref, gt_ref, xs_ref, *, n_lat):
    t = pl.program_id(2)
    has_prev = jnp.logical_and(t > 0, t != n_lat)
    has_next = t < n_lat - 1

    baf = ba_ref[0]
    lane = lax.broadcasted_iota(jnp.int32, (TM, LANE), 1)
    beta_all = jax.nn.sigmoid(baf)
    xg = baf + gp_ref[1:2, :]
    g_all = -jnp.exp(gp_ref[0:1, :]) * (jnp.maximum(xg, 0.0) + jnp.log(1.0 + jnp.exp(-jnp.abs(xg))))

    def col(a, idx):
        cvec = jnp.sum(jnp.where(lane == idx, a, 0.0), axis=1, keepdims=True)
        return jnp.broadcast_to(cvec, (TM, CH)).reshape(NCH, CH, CH)

    ii = lax.broadcasted_iota(jnp.int32, (CH, CH), 0)
    jj = lax.broadcasted_iota(jnp.int32, (CH, CH), 1)
    eye = (ii == jj).astype(F32)[None]
    nt = lambda a, b: jnp.einsum('cid,cjd->cij', a, b, preferred_element_type=F32)

    groups = []
    for hh in range(GDN_HEADS_PER_STEP):
        h = pl.program_id(1) * GDN_HEADS_PER_STEP + hh
        parts = []
        for part in range(3):
            xs_ref[hh, part, 0:8, :] = jnp.where(has_prev, xp_ref[0, part, hh], 0.0)
            xs_ref[hh, part, 8:8 + TM, :] = x_ref[0, part, hh]
            xs_ref[hh, part, 8 + TM:16 + TM, :] = jnp.where(has_next, xn_ref[0, part, hh], 0.0)
            acc = jnp.zeros((TM, GDN_DK), F32)
            for k in range(GDN_CONV):
                acc = acc + cw_ref[hh, part, k:k + 1, :] * xs_ref[hh, part, pl.ds(8 - GDN_CONV // 2 + k, TM), :]
            parts.append(acc * jax.nn.sigmoid(acc))
        q_, k_, v = parts
        q = q_ * lax.rsqrt(jnp.sum(q_ * q_, axis=-1, keepdims=True) + EPS) * (GDN_DK ** -0.5)
        k = k_ * lax.rsqrt(jnp.sum(k_ * k_, axis=-1, keepdims=True) + EPS)
        q3, k3, v3 = (a.reshape(NCH, CH, GDN_DK) for a in (q, k, v))
        kb, qb = k3.astype(BF16), q3.astype(BF16)
        kk, qk = nt(kb, kb), nt(qb, kb)
        for d in range(2):
            groups.append(dict(
                q3=q3, k3=k3, v3=v3, kk=kk, qk=qk,
                beta=col(beta_all, _BA_LANE + GDN_HEADS * d + h),
                g=col(g_all, _BA_LANE + 2 * GDN_HEADS + GDN_HEADS * d + h)))

    ngrp = len(groups)
    stack = lambda key: jnp.concatenate([grp[key] for grp in groups], axis=0)
    per_dir = lambda lo_, up_: jnp.concatenate(
        [jnp.broadcast_to((lo_ if gi % 2 == 0 else up_)[None], (NCH, CH, CH)) for gi in range(ngrp)], axis=0)
    incl = per_dir(jj <= ii, jj >= ii)
    strict = per_dir(jj < ii, jj > ii)
    q3, k3, v3, kk, qk, beta, g = (stack(key) for key in ('q3', 'k3', 'v3', 'kk', 'qk', 'beta', 'g'))
    nb = ngrp * NCH
    tri = incl.astype(BF16)
    gcx = jnp.einsum('cij,cjl->cil', jnp.concatenate([tri] * 3, axis=-1),
                     jnp.concatenate(_split3(g), axis=1), preferred_element_type=F32)
    gcr = jnp.stack([gcx[c].T for c in range(nb)], axis=0)
    decay = jnp.exp(jnp.where(incl, gcx - gcr, -jnp.inf))
    lm = jnp.where(strict, beta * kk * decay, 0.0)
    egc = jnp.exp(gcx)
    rhs = jnp.concatenate([v3 * beta, k3 * beta * egc], axis=-1)
    p = -lm
    tinv = eye + p
    for _ in range(5):
        p = _bmm3(p, p)
        tinv = tinv + _bmm3(tinv, p)
    uw = _bmm3(tinv, rhs)
    gce = jnp.concatenate(
        [gcx[NCH * gi:NCH * (gi + 1), (CH - 1 if gi % 2 == 0 else 0):(CH if gi % 2 == 0 else 1), :] for gi in range(ngrp)],
        axis=0)
    k_out = k3 * jnp.exp(gce - gcx)
    k_out_t = jnp.stack([k_out[c].T for c in range(nb)], axis=0)
    kq = jnp.concatenate([k_out_t, q3 * egc], axis=-1)
    intra = qk * decay
    gte = jnp.exp(gce)
    gtb = jnp.broadcast_to(jnp.concatenate([gte, gte], axis=-1), (nb, 8, LANE))
    for gi in range(ngrp):
        hh, d = gi // 2, gi % 2
        sl = slice(NCH * gi, NCH * (gi + 1))
        p1_ref[d, 0, hh] = uw[sl].reshape(TM, 2 * CH).astype(BF16)
        p2_ref[d, 0, hh] = kq[sl].reshape(TM, 2 * CH).astype(BF16)
        p3_ref[d, 0, hh] = intra[sl].reshape(TM, CH).astype(BF16)
        gt_ref[d, 0, hh] = gtb[sl].reshape(NCH * 8, LANE)


def _gdn_local(gx, conv_w, ba, gp, n_lat):
    B, _, T, _ = gx.shape
    H = GDN_HEADS
    hp = GDN_HEADS_PER_STEP
    nt = T // TM
    gx5 = gx.reshape(B, 3, H, T, GDN_DK)
    cw = jnp.transpose(conv_w.reshape(GDN_CONV, 3, H, GDN_DK), (2, 1, 0, 3))
    kern = functools.partial(_gdn_local_kernel, n_lat=n_lat)
    r8 = TM // 8
    big = lambda w, dt: jax.ShapeDtypeStruct((2, B, H, T, w), dt)
    ospec = lambda w: pl.BlockSpec((2, 1, hp, TM, w), lambda b, h, t: (0, b, h, t, 0))
    return pl.pallas_call(
        kern,
        grid=(B, H // hp, nt),
        in_specs=[pl.BlockSpec((1, 3, hp, TM, GDN_DK), lambda b, h, t: (b, 0, h, t, 0)),
                  pl.BlockSpec((1, 3, hp, 8, GDN_DK), lambda b, h, t: (b, 0, h, jnp.maximum(t * r8 - 1, 0), 0)),
                  pl.BlockSpec((1, 3, hp, 8, GDN_DK), lambda b, h, t: (b, 0, h, jnp.minimum((t + 1) * r8, T // 8 - 1), 0)),
                  pl.BlockSpec((hp, 3, GDN_CONV, GDN_DK), lambda b, h, t: (h, 0, 0, 0)),
                  pl.BlockSpec((1, TM, LANE), lambda b, h, t: (b, t, 0)),
                  pl.BlockSpec((2, LANE), lambda b, h, t: (0, 0))],
        out_specs=[ospec(2 * CH), ospec(2 * CH), ospec(CH),
                   pl.BlockSpec((2, 1, hp, NCH * 8, LANE), lambda b, h, t: (0, b, h, t, 0))],
        out_shape=[big(2 * CH, BF16), big(2 * CH, BF16), big(CH, BF16),
                   jax.ShapeDtypeStruct((2, B, H, (T // CH) * 8, LANE), F32)],
        scratch_shapes=[pltpu.VMEM((hp, 3, TM + 16, GDN_DK), F32)],
        compiler_params=_cparams(("parallel", "parallel", "parallel")),
        name="gdn_local",
    )(gx5, gx5, gx5, cw, ba, gp)


def _gdn_scan_kernel(p1f, p2f, p3f, gtf, p1b, p2b, p3b, gtb, of_ref, ob_ref, s_ref):
    @pl.when(pl.program_id(0) == 0)
    def _init():
        s_ref[...] = jnp.zeros_like(s_ref)

    nb, nh = s_ref.shape[1], s_ref.shape[2]
    n = nb * nh
    bmm = lambda a, b: jnp.einsum('nij,njk->nik', a, b, preferred_element_type=F32)
    new_states = []
    for d, (p1, p2, p3, gt, o_ref) in enumerate(((p1f, p2f, p3f, gtf, of_ref), (p1b, p2b, p3b, gtb, ob_ref))):
        s = s_ref[d].reshape(n, GDN_DK, GDN_DV)
        uw = p1[0].reshape(n, CH, 2 * CH)
        kq = p2[0].reshape(n, CH, 2 * CH)
        m1 = bmm(jnp.concatenate([uw[:, :, CH:], kq[:, :, CH:]], axis=1), s.astype(BF16))
        v_new = (uw[:, :, :CH].astype(F32) - m1[:, :CH]).astype(BF16)
        o = m1[:, CH:] + bmm(p3[0].reshape(n, CH, CH), v_new)
        o_ref[...] = o.reshape(nb, nh, CH, GDN_DV)
        g = gt[0].reshape(n, 8, LANE)[:, 0:1, :CH]
        new_states.append(s * g + bmm(kq[:, :, :CH], v_new))
    for d in range(2):
        s_ref[d] = new_states[d].reshape(nb, nh, GDN_DK, GDN_DV)


def _gdn_scan(p1, p2, p3, gt, S):
    _, B, H, T, _ = p1.shape
    n = T // CH
    n_lat = S // CH
    fwd = lambda i: (n_lat + i) % n
    bwd = lambda i: n - 1 - i
    specs = []
    for d, order in ((0, fwd), (1, bwd)):
        for w, rows in ((2 * CH, CH), (2 * CH, CH), (CH, CH), (LANE, 8)):
            specs.append(pl.BlockSpec((1, B, H, rows, w), functools.partial(lambda i, d, order: (d, 0, 0, order(i), 0), d=d, order=order)))
    out = jax.ShapeDtypeStruct((B, H, T, GDN_DV), F32)
    return pl.pallas_call(
        _gdn_scan_kernel,
        grid=(n,),
        in_specs=specs,
        out_specs=[pl.BlockSpec((B, H, CH, GDN_DV), lambda i: (0, 0, fwd(i), 0)),
                   pl.BlockSpec((B, H, CH, GDN_DV), lambda i: (0, 0, bwd(i), 0))],
        out_shape=[out, out],
        scratch_shapes=[pltpu.VMEM((2, B, H, GDN_DK, GDN_DV), F32)],
        compiler_params=_cparams(("arbitrary",)),
        name="gdn_scan",
    )(p1, p2, p3, gt, p1, p2, p3, gt)


def _rms(v, gain):
    return v * lax.rsqrt(jnp.mean(v * v, axis=-1, keepdims=True) + EPS) * gain


def _mix_out_kernel(x_ref, mod_ref, oa_ref, of_ref, ob_ref, z_ref, oc_ref, gw_ref, wo_ref,
                    post_ref, pre2_ref, rw_ref, rb_ref,
                    x1_ref, h2_ref, wd_ref, pos_ref, rankt_ref, cum_ref, carry_ref, *, tiles_per_super):
    g = pl.program_id(0) * pl.num_programs(1) + pl.program_id(1)

    @pl.when(g % tiles_per_super == 0)
    def _reset():
        carry_ref[...] = jnp.zeros_like(carry_ref)

    z = z_ref[0]
    gated = []
    for h in range(GDN_HEADS):
        o = of_ref[0, h] + ob_ref[0, h]
        zh = z[:, GDN_DV * h:GDN_DV * (h + 1)]
        gated.append(_rms(o, gw_ref[...]) * (zh * jax.nn.sigmoid(zh)))
    mixed = jnp.concatenate([oa_ref[0], jnp.concatenate(gated, axis=1).astype(BF16), oc_ref[0]], axis=1)
    y = jnp.dot(mixed, wo_ref[...], preferred_element_type=F32)
    x1 = x_ref[0] + mod_ref[0, 2:3, :] * _rms(y, post_ref[...])
    x1_ref[0] = x1
    h2 = (_rms(x1, pre2_ref[...]) * (1.0 + mod_ref[0, 4:5, :]) + mod_ref[0, 3:4, :]).astype(BF16)
    h2_ref[...] = h2

    scores = jax.nn.sigmoid(_nt_dot(rw_ref[...], h2))
    sel = scores + rb_ref[...]
    gsz = N_EXPERTS // MOE_GROUPS
    g3 = sel.reshape(MOE_GROUPS, gsz, TM)
    io = lax.broadcasted_iota(jnp.int32, g3.shape, 1)
    m1 = jnp.max(g3, axis=1, keepdims=True)
    i1 = jnp.min(jnp.where(g3 == m1, io, gsz), axis=1, keepdims=True)
    m2 = jnp.max(jnp.where(io == i1, -jnp.inf, g3), axis=1, keepdims=True)

    def top_mask(vals, k):
        n = vals.shape[0]
        idx = lax.broadcasted_iota(jnp.int32, vals.shape, 0)
        mask = jnp.zeros(vals.shape, F32)
        for _ in range(k):
            mx = jnp.max(vals, axis=0, keepdims=True)
            hit = idx == jnp.min(jnp.where(vals == mx, idx, n), axis=0, keepdims=True)
            mask = jnp.where(hit, 1.0, mask)
            vals = jnp.where(hit, -jnp.inf, vals)
        return mask

    gmask = top_mask((m1 + m2).reshape(MOE_GROUPS, TM), MOE_TOPK_GROUPS)
    masked = jnp.where(gmask.reshape(MOE_GROUPS, 1, TM) > 0.0, g3, -jnp.inf).reshape(N_EXPERTS, TM)
    smask = top_mask(masked, MOE_TOP_K)
    w = jnp.where(smask > 0.0, scores, 0.0)
    wn = w / jnp.sum(w, axis=0, keepdims=True) * ROUTED_SCALE
    wd_ref[...] = jnp.where(smask > 0.0, wn, -1.0)

    ci = lax.broadcasted_iota(jnp.int32, (TM, TM), 0)
    cj = lax.broadcasted_iota(jnp.int32, (TM, TM), 1)
    ut = jnp.where(ci <= cj, 1.0, 0.0).astype(BF16)
    cs = jnp.dot(smask.astype(BF16), ut, preferred_element_type=F32)
    carry = carry_ref[...]
    posf = jnp.where(smask > 0.0, carry + cs - 1.0, -1.0)
    pos_ref[...] = posf.astype(jnp.int32)
    rankt_ref[...] = posf.T
    carry = carry + jnp.broadcast_to(cs[:, TM - 1:TM], (N_EXPERTS, TM))
    carry_ref[...] = carry
    cum_ref[0] = carry[:, :LANE]


def _mix_out(xx, modl, oa, o_f, o_b, z, oc, gw, wo, post, pre2, rw, rb, n_lat, nt, tiles_per_super):
    B, T, D = xx.shape
    nb = modl.shape[0] - 1
    N = B * nt * TM

    def mod_map(b, t):
        return (jnp.where(t >= n_lat, nb, b), 0, 0)

    row = lambda w: pl.BlockSpec((1, TM, w), lambda b, t: (b, t, 0))
    flat = lambda b, t: (0, b * nt + t)
    kern = functools.partial(_mix_out_kernel, tiles_per_super=tiles_per_super)
    return pl.pallas_call(
        kern,
        grid=(B, nt),
        in_specs=[row(D), pl.BlockSpec((1, 6, D), mod_map), row(384),
                  pl.BlockSpec((1, GDN_HEADS, TM, GDN_DV), lambda b, t: (b, 0, t, 0)),
                  pl.BlockSpec((1, GDN_HEADS, TM, GDN_DV), lambda b, t: (b, 0, t, 0)),
                  row(384), row(256), _const_spec((1, GDN_DV)), _const_spec(wo.shape),
                  _const_spec((1, D)), _const_spec((1, D)), _const_spec(rw.shape), _const_spec((N_EXPERTS, 1))],
        out_specs=[row(D),
                   pl.BlockSpec((TM, D), lambda b, t: (b * nt + t, 0)),
                   pl.BlockSpec((N_EXPERTS, TM), flat),
                   pl.BlockSpec((N_EXPERTS, TM), flat),
                   pl.BlockSpec((TM, N_EXPERTS), lambda b, t: (b * nt + t, 0)),
                   pl.BlockSpec((1, N_EXPERTS, LANE), lambda b, t: (b * nt + t, 0, 0))],
        out_shape=[jax.ShapeDtypeStruct((B, nt * TM, D), F32),
                   jax.ShapeDtypeStruct((N, D), BF16),
                   jax.ShapeDtypeStruct((N_EXPERTS, N), F32),
                   jax.ShapeDtypeStruct((N_EXPERTS, N), jnp.int32),
                   jax.ShapeDtypeStruct((N, N_EXPERTS), F32),
                   jax.ShapeDtypeStruct((N // TM, N_EXPERTS, LANE), F32)],
        scratch_shapes=[pltpu.VMEM((N_EXPERTS, TM), F32)],
        compiler_params=_cparams(("arbitrary", "arbitrary")),
        name="mix_out_router",
    )(xx, modl, oa, o_f, o_b, z, oc, gw, wo, post, pre2, rw, rb)


MOE_BLOCK = 128


MOE_SUPER_TILES = 3
MOE_CHUNK_BLOCKS = 8
MOE_NO_RANK = 1 << 20


def _moe_max_blocks(tps):
    b = tps * TM * MOE_TOP_K // MOE_BLOCK + N_EXPERTS
    return -(-b // MOE_CHUNK_BLOCKS) * MOE_CHUNK_BLOCKS


def _moe2_meta(cum, n_tiles, tps):
    ns = -(-n_tiles // tps)
    maxblk = _moe_max_blocks(tps)
    last = jnp.minimum((jnp.arange(ns) + 1) * tps - 1, n_tiles - 1)
    cnt = cum[last, :, 0].astype(jnp.int32)
    nslot = (cnt + MOE_BLOCK - 1) // MOE_BLOCK
    bend = jnp.cumsum(nslot, axis=1)
    bstart = bend - nslot
    b = jnp.arange(maxblk, dtype=jnp.int32)
    blk_e = jnp.minimum(jnp.sum(bend[:, None, :] <= b[None, :, None], axis=2), N_EXPERTS - 1)
    blk_k = b[None, :] - jnp.take_along_axis(bstart, blk_e, axis=1)
    blk_r0 = jnp.where(b[None, :] < bend[:, -1:], blk_k * MOE_BLOCK, MOE_NO_RANK)
    i32 = lambda a: a.reshape(-1).astype(jnp.int32)
    return i32(nslot), i32(bstart), i32(bend[:, -1]), i32(blk_e), i32(blk_r0)


def _moe2_kernel(nslot_ref, bstart_ref, nblk_ref, blke_ref, blkr_ref,
                 h2_ref, pos_ref, rankt_ref, wd_ref, wg_ref, wu_ref, wdn_ref, out_ref, yw_ref,
                 *, tps, maxblk, n_tiles):
    s = pl.program_id(0)
    j = pl.program_id(1)
    st = tps * TM

    @pl.when(jnp.logical_and(s == 0, j == 0))
    def _init():
        yw_ref[...] = jnp.zeros_like(yw_ref)

    @pl.when(j < N_EXPERTS)
    def _expert():
        p = s * N_EXPERTS + j
        prow = pos_ref[0]
        wrow = wd_ref[0]
        riota = lax.broadcasted_iota(jnp.int32, (MOE_BLOCK, st), 0)

        def slot(k, c):
            hit = prow == riota + k * MOE_BLOCK
            x = jnp.dot(jnp.where(hit, 1.0, 0.0).astype(BF16), h2_ref[...], preferred_element_type=F32).astype(BF16)
            wcol = jnp.sum(jnp.where(hit, wrow, 0.0), axis=1, keepdims=True)
            hg = jnp.dot(x, wg_ref[0], preferred_element_type=F32)
            hu = jnp.dot(x, wu_ref[0], preferred_element_type=F32)
            y = jnp.dot((hg * jax.nn.sigmoid(hg) * hu).astype(BF16), wdn_ref[0], preferred_element_type=F32)
            row0 = pl.multiple_of((bstart_ref[p] + k) * MOE_BLOCK, MOE_BLOCK)
            yw_ref[pl.ds(row0, MOE_BLOCK), :] = (y * wcol).astype(BF16)
            return c

        lax.fori_loop(0, nslot_ref[p], slot, 0)

    @pl.when(jnp.logical_and(j >= N_EXPERTS, s * tps + j - N_EXPERTS < n_tiles))
    def _combine():
        t = j - N_EXPERTS
        pt = rankt_ref[pl.ds(pl.multiple_of(t * TM, TM), TM), :]
        lane_e = lax.broadcasted_iota(jnp.int32, (TM, N_EXPERTS), 1)
        li = lax.broadcasted_iota(jnp.int32, (TM, MOE_BLOCK), 1)
        rows = MOE_CHUNK_BLOCKS * MOE_BLOCK

        def chunk(c, acc):
            hts = []
            for bi in range(MOE_CHUNK_BLOCKS):
                b = s * maxblk + c * MOE_CHUNK_BLOCKS + bi
                col = jnp.sum(jnp.where(lane_e == blke_ref[b], pt, 0.0), axis=1, keepdims=True)
                hts.append(jnp.where(col == (li + blkr_ref[b]).astype(F32), 1.0, 0.0).astype(BF16))
            ht = jnp.concatenate(hts, axis=1)
            return acc + jnp.dot(ht, yw_ref[pl.ds(pl.multiple_of(c * rows, rows), rows), :],
                                 preferred_element_type=F32)

        n_chunks = (nblk_ref[s] + MOE_CHUNK_BLOCKS - 1) // MOE_CHUNK_BLOCKS
        out_ref[...] = lax.fori_loop(0, n_chunks, chunk, jnp.zeros(out_ref.shape, F32))


def _moe2_routed(h2, wd, pos, rankt, cum, wg, wu, wdn):
    N, D = h2.shape
    E = N_EXPERTS
    n_tiles = N // TM
    tps = min(MOE_SUPER_TILES, n_tiles)
    ns = -(-n_tiles // tps)
    st = tps * TM
    pad = ns * st - N
    if pad:
        h2 = jnp.pad(h2, ((0, pad), (0, 0)))
        wd = jnp.pad(wd, ((0, 0), (0, pad)))
        pos = jnp.pad(pos, ((0, 0), (0, pad)), constant_values=-1)
        rankt = jnp.pad(rankt, ((0, pad), (0, 0)), constant_values=-1.0)
    maxblk = _moe_max_blocks(tps)
    meta = _moe2_meta(cum, n_tiles, tps)
    ex = lambda j: jnp.minimum(j, E - 1)
    grid_spec = pltpu.PrefetchScalarGridSpec(
        num_scalar_prefetch=5,
        grid=(ns, E + tps),
        in_specs=[pl.BlockSpec((st, D), lambda s, j, *_: (s, 0)),
                  pl.BlockSpec((1, 1, st), lambda s, j, *_: (ex(j), 0, s)),
                  pl.BlockSpec((st, E), lambda s, j, *_: (s, 0)),
                  pl.BlockSpec((1, 1, st), lambda s, j, *_: (ex(j), 0, s)),
                  pl.BlockSpec((1, D, D_EXPERT), lambda s, j, *_: (ex(j), 0, 0)),
                  pl.BlockSpec((1, D, D_EXPERT), lambda s, j, *_: (ex(j), 0, 0)),
                  pl.BlockSpec((1, D_EXPERT, D), lambda s, j, *_: (ex(j), 0, 0))],
        out_specs=pl.BlockSpec((TM, D), lambda s, j, *_: (jnp.minimum(s * tps + jnp.maximum(j - E, 0), n_tiles - 1), 0)),
        scratch_shapes=[pltpu.VMEM((maxblk * MOE_BLOCK, D), BF16)],
    )
    return pl.pallas_call(
        functools.partial(_moe2_kernel, tps=tps, maxblk=maxblk, n_tiles=n_tiles),
        grid_spec=grid_spec,
        out_shape=jax.ShapeDtypeStruct((N, D), F32),
        compiler_params=_cparams(("arbitrary", "arbitrary")),
        name="moe_routed",
    )(*meta, h2, pos.reshape(E, 1, ns * st), rankt, wd.reshape(E, 1, ns * st), wg, wu, wdn)


def _ffn_out_kernel(x1_ref, mod_ref, h2_ref, routed_ref, sg_ref, su_ref, sd_ref, post_ref, o_ref):
    h2 = h2_ref[...]
    hg = jnp.dot(h2, sg_ref[...], preferred_element_type=F32)
    hu = jnp.dot(h2, su_ref[...], preferred_element_type=F32)
    f = jnp.dot((hg * jax.nn.sigmoid(hg) * hu).astype(BF16), sd_ref[...], preferred_element_type=F32) + routed_ref[...]
    o_ref[0] = x1_ref[0] + mod_ref[0, 5:6, :] * _rms(f, post_ref[...])


def _ffn_out(x1, modl, h2, routed, sg, su, sd, post, n_lat):
    B, Tn, D = x1.shape
    nt = Tn // TM
    nb = modl.shape[0] - 1

    def mod_map(b, t):
        return (jnp.where(t >= n_lat, nb, b), 0, 0)

    row = pl.BlockSpec((1, TM, D), lambda b, t: (b, t, 0))
    flat = pl.BlockSpec((TM, D), lambda b, t: (b * nt + t, 0))
    return pl.pallas_call(
        _ffn_out_kernel,
        grid=(B, nt),
        in_specs=[row, pl.BlockSpec((1, 6, D), mod_map), flat, flat,
                  _const_spec(sg.shape), _const_spec(su.shape), _const_spec(sd.shape), _const_spec((1, D))],
        out_specs=row,
        out_shape=jax.ShapeDtypeStruct((B, Tn, D), F32),
        compiler_params=_cparams(("parallel", "parallel")),
        name="ffn_out",
    )(x1, modl, h2, routed, sg, su, sd, post)


def kernel(x, c, ctx, c_ctx, ada_w, ada_b, mix_norm_pre, mix_norm_post, ffn_norm_pre, ffn_norm_post, w_in, w_out, swa_sink, gdn_conv_w, gdn_a_log, gdn_dt_bias, gdn_norm_w, mla_q_norm, mla_w_uq, mla_kv_norm, mla_w_ukv, router_w, router_bias, expert_w_gate, expert_w_up, expert_w_down, shared_w_gate, shared_w_up, shared_w_down):
    B, S, D = x.shape
    L = ctx.shape[1]
    assert L == TM and S % (2 * TM) == 0
    T = S + L
    n_lat = S // TM
    xx = jnp.concatenate([x, ctx], axis=1)
    tabs = _rope_tables(S, T)
    cvecs = jnp.concatenate([c, c_ctx[None], jnp.zeros((8 - B - 1, D), F32)], axis=0)
    mods = _modulation(cvecs, ada_w, ada_b).reshape(DEPTH, 8, 6, D)[:, :B + 1]
    for layer in range(DEPTH):
        xx = _layer(layer, xx, mods[layer], tabs, S, layer == DEPTH - 1,
                    mix_norm_pre, mix_norm_post, ffn_norm_pre, ffn_norm_post, w_in, w_out, swa_sink, gdn_conv_w,
                    gdn_a_log, gdn_dt_bias, gdn_norm_w, mla_q_norm, mla_w_uq, mla_kv_norm, mla_w_ukv, router_w,
                    router_bias, expert_w_gate, expert_w_up, expert_w_down, shared_w_gate, shared_w_up, shared_w_down)
    return xx


def _layer(layer, xx, modl, tabs, S, last, mix_norm_pre, mix_norm_post, ffn_norm_pre, ffn_norm_post, w_in, w_out,
           swa_sink, gdn_conv_w, gdn_a_log, gdn_dt_bias, gdn_norm_w, mla_q_norm, mla_w_uq, mla_kv_norm, mla_w_ukv,
           router_w, router_bias, expert_w_gate, expert_w_up, expert_w_down, shared_w_gate, shared_w_up, shared_w_down):
    B, T, D = xx.shape
    n_lat = S // TM
    nt = n_lat if last else T // TM
    qa, ka, va, gx, z, ba, qm, km, vm = _project(
        xx, modl, mix_norm_pre[layer][None], _prep_w_in(w_in[layer]), _prep_w_uq(mla_w_uq[layer]),
        _prep_w_ukv(mla_w_ukv[layer]), jnp.pad(mla_q_norm[layer], (0, 256 - MLA_Q_RANK))[None],
        mla_kv_norm[layer][None], tabs, n_lat)
    oa = _swa_attention(swa_sink[layer], qa, ka, va, S, nt)
    oc = _mla_attention(qm, km, vm, S, nt)
    gp = jnp.zeros((2, LANE), F32)
    g0 = _BA_LANE + 2 * GDN_HEADS
    gp = gp.at[0, g0:g0 + 2 * GDN_HEADS].set(gdn_a_log[layer].reshape(-1))
    gp = gp.at[1, g0:g0 + 2 * GDN_HEADS].set(gdn_dt_bias[layer].reshape(-1))
    p1, p2, p3, gt = _gdn_local(gx, gdn_conv_w[layer], ba, gp, n_lat)
    o_f, o_b = _gdn_scan(p1, p2, p3, gt, S)

    x1, h2, wd, pos, rankt, cum = _mix_out(
        xx, modl, oa, o_f, o_b, z, oc, gdn_norm_w[layer][None], _prep_w_out(w_out[layer]),
        mix_norm_post[layer][None], ffn_norm_pre[layer][None], router_w[layer].T.astype(BF16),
        router_bias[layer][:, None], n_lat, nt, min(MOE_SUPER_TILES, B * nt))
    routed = _moe2_routed(h2, wd, pos, rankt, cum, expert_w_gate[layer].astype(BF16),
                          expert_w_up[layer].astype(BF16), expert_w_down[layer].astype(BF16))
    return _ffn_out(x1, modl, h2, routed, shared_w_gate[layer].astype(BF16), shared_w_up[layer].astype(BF16),
                    shared_w_down[layer].astype(BF16), ffn_norm_post[layer][None], n_lat)
```

```python
import functools
import math

import numpy as np
import jax
import jax.numpy as jnp
from jax import lax
from jax.experimental import pallas as pl
from jax.experimental.pallas import tpu as pltpu

F32 = jnp.float32
BF16 = jnp.bfloat16

DEPTH = 2
GRID_W = 64
EPS = 1e-6
ROPE_BASE = 10000.0
HEAD_DIM = 64
SWA_HEADS = 6
SWA_KV_HEADS = 2
SWA_WINDOW = 128
GDN_HEADS = 6
GDN_DK = 64
GDN_DV = 64
GDN_CONV = 5
GDN_CHUNK = 64
MLA_HEADS = 4
MLA_Q_RANK = 192
MLA_KV_RANK = 128
MLA_NOPE = 64
MLA_ROPE = 32
MLA_V = 64
N_EXPERTS = 64
MOE_TOP_K = 8
MOE_GROUPS = 8
MOE_TOPK_GROUPS = 4
D_EXPERT = 256
ROUTED_SCALE = 2.5
LOG2_E = math.log2(math.e)

_SPLITS = (384, 128, 128, 1152, 384, 24, 192, 128, 32)
_OFF = np.concatenate([[0], np.cumsum(_SPLITS)]).tolist()
D_PROJ = _OFF[-1]

TM = 256
LANE = 128
VMEM_LIMIT = 56 * 1024 * 1024

_C_QA, _C_QAS, _C_KA, _C_KAS, _C_VA, _C_GDN, _C_Z, _C_CQ, _C_CKV, _C_KRB, _C_END = (
    0, 768, 1536, 1664, 1792, 1920, 3072, 3456, 3712, 3840, 3968)
_BA_LANE = 32


def _cparams(sem, vmem=VMEM_LIMIT):
    return pltpu.CompilerParams(dimension_semantics=sem, vmem_limit_bytes=vmem)


def _rope_partner(d, width):
    half, n = width // 2, width // 4
    i = d % half
    return (d // half) * half + (i + n if i < n else i - n)


def _take_cols(w, idx):
    idx = [int(i) for i in idx]
    pieces, start = [], 0
    for pos in range(1, len(idx) + 1):
        run_ends = pos == len(idx) or (idx[pos] != idx[pos - 1] + 1 if idx[pos - 1] >= 0 else idx[pos] >= 0) \
            or (idx[pos] < 0) != (idx[pos - 1] < 0)
        if run_ends:
            n = pos - start
            pieces.append(jnp.zeros((w.shape[0], n), w.dtype) if idx[start] < 0 else w[:, idx[start]:idx[start] + n])
            start = pos
    return jnp.concatenate(pieces, axis=1)


def _prep_w_in(w_in):
    cols = []
    for swap in (False, True):
        for h in range(SWA_HEADS):
            j = h // (SWA_HEADS // SWA_KV_HEADS)
            blk = [-1] * LANE
            for d in range(HEAD_DIM):
                blk[64 * j + d] = _OFF[0] + h * HEAD_DIM + (_rope_partner(d, HEAD_DIM) if swap else d)
            cols += blk
    for swap in (False, True):
        for j in range(SWA_KV_HEADS):
            cols += [_OFF[1] + j * HEAD_DIM + (_rope_partner(d, HEAD_DIM) if swap else d) for d in range(HEAD_DIM)]
    cols += list(range(_OFF[2], _OFF[3]))
    cols += list(range(_OFF[3], _OFF[4]))
    cols += list(range(_OFF[4], _OFF[5]))
    cols += list(range(_OFF[6], _OFF[7])) + [-1] * 64
    cols += list(range(_OFF[7], _OFF[8]))
    cols += list(range(_OFF[8], _OFF[9])) + list(range(_OFF[5], _OFF[6])) + [-1] * (LANE - 32 - 24)
    assert len(cols) == _C_END
    return _take_cols(w_in, cols).astype(BF16)


def _prep_w_uq(w_uq):
    cols = []
    for swap in (False, True):
        for h in range(MLA_HEADS):
            base = h * (MLA_NOPE + MLA_ROPE)
            blk = [-1] * LANE
            for d in range(MLA_NOPE):
                blk[d] = -1 if swap else base + d
            for r in range(MLA_ROPE):
                blk[MLA_NOPE + r] = base + MLA_NOPE + (_rope_partner(r, MLA_ROPE) if swap else r)
            cols += blk
    w = _take_cols(w_uq, cols)
    return jnp.pad(w, ((0, 256 - MLA_Q_RANK), (0, 0))).astype(BF16)


def _prep_w_ukv(w_ukv):
    kcols, vcols = [], []
    for h in range(MLA_HEADS):
        base = h * (MLA_NOPE + MLA_V)
        kcols += [base + d for d in range(MLA_NOPE)] + [-1] * 64
        vcols += [base + MLA_NOPE + d for d in range(MLA_V)]
    wk = _take_cols(w_ukv, kcols)
    wv = _take_cols(w_ukv, vcols)
    top = jnp.concatenate([wk, jnp.zeros_like(wk), wv], axis=1)
    place = np.zeros((128, 1280), np.float32)
    for h in range(MLA_HEADS):
        for r in range(MLA_ROPE):
            place[r, 128 * h + MLA_NOPE + r] = 1.0
            place[_rope_partner(r, MLA_ROPE), 512 + 128 * h + MLA_NOPE + r] = 1.0
    return jnp.concatenate([top, jnp.asarray(place)], axis=0).astype(BF16)


def _prep_w_out(w_out):
    rows = []
    G = SWA_HEADS // SWA_KV_HEADS
    for g in range(G):
        for j in range(SWA_KV_HEADS):
            rows += [(G * j + g) * HEAD_DIM + d for d in range(HEAD_DIM)]
    rows += list(range(SWA_HEADS * HEAD_DIM, w_out.shape[0]))
    return jnp.take(w_out, jnp.asarray(rows), axis=0).astype(BF16)


def _rope_tables(S, T):
    t = np.arange(S)
    row, col = t // GRID_W, t % GRID_W

    def tab(width, lanes):
        half, n = width // 2, width // 4
        c = np.ones((T, LANE), np.float64)
        s = np.zeros((T, LANE), np.float64)
        for lane, d in lanes:
            i = d % half
            pos = row if d < half else col
            ang =(pos.astype(np.float32) * np.float32(ROPE_BASE ** (-(i % n) / n))).astype(np.float64)
            c[:S, lane] = np.cos(ang)
            s[:S, lane] = -np.sin(ang) if i < n else np.sin(ang)
        return jnp.asarray(c, F32), jnp.asarray(s, F32)

    ca, sa = tab(HEAD_DIM, [(l, l % HEAD_DIM) for l in range(LANE)])
    cm, sm = tab(MLA_ROPE, [(MLA_NOPE + r, r) for r in range(MLA_ROPE)])
    return ca, sa, cm, sm


def _mod_kernel(c_ref, w_ref, b_ref, o_ref):
    cv = c_ref[...]
    a = (cv * jax.nn.sigmoid(cv)).astype(BF16)
    o_ref[0] = jnp.dot(a, w_ref[0].astype(BF16), preferred_element_type=F32) + b_ref[0]


def _modulation(cvecs, ada_w, ada_b):
    depth, D, N = ada_w.shape
    tn = 512
    return pl.pallas_call(
        _mod_kernel,
        grid=(depth, N // tn),
        in_specs=[pl.BlockSpec((8, D), lambda l, j: (0, 0)),
                  pl.BlockSpec((1, D, tn), lambda l, j: (l, 0, j)),
                  pl.BlockSpec((1, 1, tn), lambda l, j: (l, 0, j))],
        out_specs=pl.BlockSpec((1, 8, tn), lambda l, j: (l, 0, j)),
        out_shape=jax.ShapeDtypeStruct((depth, 8, N), F32),
        compiler_params=_cparams(("arbitrary", "arbitrary")),
        name="modulation",
    )(cvecs, ada_w, ada_b.reshape(depth, 1, N))


def _proj_kernel(x_ref, mod_ref, gain_ref, w_ref, wq2_ref, wk2_ref, qg_ref, kvg_ref,
                 ca_ref, sa_ref, cm_ref, sm_ref,
                 qa_ref, ka_ref, va_ref, gx_ref, z_ref, ba_ref, qm_ref, km_ref, vm_ref):
    x = x_ref[0]
    ms = jnp.mean(x * x, axis=-1, keepdims=True)
    h = x * lax.rsqrt(ms + EPS) * gain_ref[...]
    h = h * (1.0 + mod_ref[0, 1:2, :]) + mod_ref[0, 0:1, :]
    hb = h.astype(BF16)

    ca, sa, cm, sm = ca_ref[...], sa_ref[...], cm_ref[...], sm_ref[...]

    def rope(a, b, c, s):
        n = a.shape[1] // LANE
        return a * jnp.concatenate([c] * n, axis=1) + b * jnp.concatenate([s] * n, axis=1)

    p1 = jnp.dot(hb, w_ref[:, _C_QA:_C_GDN], preferred_element_type=F32)
    qa = rope(p1[:, _C_QA:_C_QAS], p1[:, _C_QAS:_C_KA], ca, sa)
    qa_ref[0] = (qa * (HEAD_DIM ** -0.5)).astype(BF16)
    ka_ref[0] = rope(p1[:, _C_KA:_C_KAS], p1[:, _C_KAS:_C_VA], ca, sa).astype(BF16)
    va_ref[0] = p1[:, _C_VA:_C_GDN].astype(BF16)

    p2 = jnp.dot(hb, w_ref[:, _C_GDN:_C_Z], preferred_element_type=F32)
    for j in range(3 * GDN_HEADS):
        gx_ref[0, j] = p2[:, 64 * j:64 * j + 64]

    p3 = jnp.dot(hb, w_ref[:, _C_Z:_C_END], preferred_element_type=F32)
    z_ref[0] = p3[:, 0:_C_CQ - _C_Z]
    cq = p3[:, _C_CQ - _C_Z:_C_CKV - _C_Z]
    ckv = p3[:, _C_CKV - _C_Z:_C_KRB - _C_Z]
    krb = p3[:, _C_KRB - _C_Z:]
    ba_ref[0] = krb

    cqn = cq * lax.rsqrt(jnp.sum(cq * cq, axis=-1, keepdims=True) * (1.0 / MLA_Q_RANK) + EPS) * qg_ref[...]
    e = jnp.dot(cqn.astype(BF16), wq2_ref[...], preferred_element_type=F32)
    qm = rope(e[:, :512], e[:, 512:], cm, sm)
    qm_ref[0] = (qm * (LOG2_E * (MLA_NOPE + MLA_ROPE) ** -0.5)).astype(BF16)

    ckvn = ckv * lax.rsqrt(jnp.mean(ckv * ckv, axis=-1, keepdims=True) + EPS) * kvg_ref[...]
    lhs2 = jnp.concatenate([ckvn.astype(BF16), krb.astype(BF16)], axis=1)
    e2 = jnp.dot(lhs2, wk2_ref[...], preferred_element_type=F32)
    km_ref[0] = rope(e2[:, :512], e2[:, 512:1024], cm, sm).astype(BF16)
    vm_ref[0] = e2[:, 1024:].T.astype(BF16)


def _const_spec(shape):
    nd = len(shape)
    return pl.BlockSpec(shape, lambda *_: (0,) * nd)


def _project(xx, modl, gain, w_main, wq2, wk2, qg, kvg, tabs, n_lat_tiles):
    B, T, D = xx.shape
    nt = T // TM
    nb = modl.shape[0] - 1

    def mod_map(b, t):
        return (jnp.where(t >= n_lat_tiles, nb, b), 0, 0)

    row = lambda w: pl.BlockSpec((1, TM, w), lambda b, t: (b, t, 0))
    tab = pl.BlockSpec((TM, LANE), lambda b, t: (t, 0))
    out_shapes = [
        jax.ShapeDtypeStruct((B, T, 768), BF16),
        jax.ShapeDtypeStruct((B, T, 128), BF16),
        jax.ShapeDtypeStruct((B, T, 128), BF16),
        jax.ShapeDtypeStruct((B, 18, T, 64), F32),
        jax.ShapeDtypeStruct((B, T, 384), F32),
        jax.ShapeDtypeStruct((B, T, 128), F32),
        jax.ShapeDtypeStruct((B, T, 512), BF16),
        jax.ShapeDtypeStruct((B, T, 512), BF16),
        jax.ShapeDtypeStruct((B, 256, T), BF16),
    ]
    out_specs = [row(768), row(128), row(128),
                 pl.BlockSpec((1, 18, TM, 64), lambda b, t: (b, 0, t, 0)),
                 row(384), row(128), row(512), row(512), pl.BlockSpec((1, 256, TM), lambda b, t: (b, 0, t))]
    return pl.pallas_call(
        _proj_kernel,
        grid=(B, nt),
        in_specs=[row(D), pl.BlockSpec((1, 6, D), mod_map), _const_spec((1, D)),
                  _const_spec(w_main.shape), _const_spec(wq2.shape), _const_spec(wk2.shape),
                  _const_spec((1, 256)), _const_spec((1, 128)), tab, tab, tab, tab],
        out_specs=out_specs,
        out_shape=out_shapes,
        compiler_params=_cparams(("parallel", "parallel")),
        name="in_proj",
    )(xx, modl, gain, w_main, wq2, wk2, qg, kvg, *tabs)


def _nt_dot(a, b):
    return lax.dot_general(a, b, (((1,), (1,)), ((), ())), preferred_element_type=F32)


def _swa_kernel(sink_ref, q_ref, k_ref, v_ref, o_ref, *, S, n_lat):
    i = pl.program_id(1)
    G = SWA_HEADS // SWA_KV_HEADS
    W = 2 * TM
    lane = lax.broadcasted_iota(jnp.int32, (TM, LANE), 1)
    kc = k_ref[0, pl.ds(S, TM), :]
    vc = v_ref[0, pl.ds(S, TM), :]

    def heads(local):
        outs = []
        for g in range(G):
            og = []
            for j in range(SWA_KV_HEADS):
                h = G * j + g
                q = q_ref[0, :, LANE * h:LANE * (h + 1)]
                sink = sink_ref[h]
                s_ctx = _nt_dot(q, kc)
                m = jnp.maximum(jnp.max(s_ctx, axis=-1, keepdims=True), sink)
                if local is not None:
                    kw, vw, valid = local
                    s_loc = jnp.where(valid, _nt_dot(q, kw), -jnp.inf)
                    m = jnp.maximum(m, jnp.max(s_loc, axis=-1, keepdims=True))
                p_ctx = jnp.exp(s_ctx - m)
                den = jnp.sum(p_ctx, axis=-1, keepdims=True) + jnp.exp(sink - m)
                o = jnp.dot(p_ctx.astype(BF16), vc, preferred_element_type=F32)
                if local is not None:
                    p_loc = jnp.exp(s_loc - m)
                    den = den + jnp.sum(p_loc, axis=-1, keepdims=True)
                    o = o + jnp.dot(p_loc.astype(BF16), vw, preferred_element_type=F32)
                og.append(o * (1.0 / den))
            outs.append(jnp.where(lane < HEAD_DIM, og[0], og[1]))
        o_ref[0] = jnp.concatenate(outs, axis=1).astype(BF16)

    @pl.when(i < n_lat)
    def _latent():
        start = pl.multiple_of(jnp.clip(i * TM - SWA_WINDOW, 0, S - W), LANE)
        kw = k_ref[0, pl.ds(start, W), :]
        vw = v_ref[0, pl.ds(start, W), :]
        qpos = i * TM + lax.broadcasted_iota(jnp.int32, (TM, W), 0)
        kpos = start + lax.broadcasted_iota(jnp.int32, (TM, W), 1)
        heads((kw, vw, jnp.abs(qpos - kpos) <= SWA_WINDOW))

    @pl.when(i >= n_lat)
    def _context():
        heads(None)


def _swa_attention(sink, qa, ka, va, S, nt):
    B, T, _ = qa.shape
    kern = functools.partial(_swa_kernel, S=S, n_lat=S // TM)
    return pl.pallas_call(
        kern,
        grid=(B, nt),
        in_specs=[pl.BlockSpec(memory_space=pltpu.SMEM),
                  pl.BlockSpec((1, TM, 768), lambda b, t: (b, t, 0)),
                  pl.BlockSpec((1, T, 128), lambda b, t: (b, 0, 0)),
                  pl.BlockSpec((1, T, 128), lambda b, t: (b, 0, 0))],
        out_specs=pl.BlockSpec((1, TM, 384), lambda b, t: (b, t, 0)),
        out_shape=jax.ShapeDtypeStruct((B, nt * TM, 384), BF16),
        compiler_params=_cparams(("parallel", "parallel")),
        name="swa_attention",
    )(sink, qa, ka, va)


MLA_SUBSTEPS = 2


def _mla_kernel(q_ref, k_ref, vt_ref, o_ref, *, S, tq, latent):
    lane = lax.broadcasted_iota(jnp.int32, (tq, LANE), 1)
    chunk = max(c for c in (2048, 1024, 512) if S % c == 0)
    n_chunks = S // chunk if latent else 0
    sub = chunk // MLA_SUBSTEPS
    ctx_keys = TM

    outs = []
    for h in range(MLA_HEADS):
        q = q_ref[0, :, LANE * h:LANE * (h + 1)]

        def step(off, size, carry):
            m, l, acc = carry
            st = _nt_dot(k_ref[0, pl.ds(off, size), LANE * h:LANE * (h + 1)], q)
            m_new = jnp.maximum(m, jnp.max(jnp.max(st.reshape(size // 8, 8, tq), axis=0), axis=0, keepdims=True))
            alpha = jnp.exp2(m - m_new)
            p = jnp.exp2(st - m_new)
            l = alpha * l + jnp.sum(p.reshape(size // 8, 8, tq), axis=0)
            vt = vt_ref[0, LANE * (h // 2):LANE * (h // 2 + 1), pl.ds(off, size)]
            return m_new, l, alpha * acc + jnp.dot(vt, p.astype(BF16), preferred_element_type=F32)

        def body(c, cr):
            off = pl.multiple_of(c * chunk, chunk)
            for u in range(MLA_SUBSTEPS):
                cr = step(pl.multiple_of(off + u * sub, TM), sub, cr)
            return cr

        carry = (jnp.full((1, tq), -jnp.inf, F32), jnp.zeros((8, tq), F32), jnp.zeros((LANE, tq), F32))
        if n_chunks:
            carry = lax.fori_loop(0, n_chunks, body, carry)
        m, l, acc = step(S, ctx_keys, carry)
        outs.append((acc * (1.0 / jnp.sum(l, axis=0, keepdims=True))).T)
    o_ref[0] = jnp.concatenate([jnp.where(lane < MLA_V, outs[0], outs[1]),
                                jnp.where(lane < MLA_V, outs[2], outs[3])], axis=1).astype(BF16)


MLA_Q_TILE = 512


def _mla_attention(qm, km, vm, S, with_context):
    B, T, _ = qm.shape

    def call(tq, n_q, q_block0, latent):
        return pl.pallas_call(
            functools.partial(_mla_kernel, S=S, tq=tq, latent=latent),
            grid=(B, n_q),
            in_specs=[pl.BlockSpec((1, tq, 512), lambda b, t: (b, q_block0 + t, 0)),
                      pl.BlockSpec((1, T, 512), lambda b, t: (b, 0, 0)),
                      pl.BlockSpec((1, 256, T), lambda b, t: (b, 0, 0))],
            out_specs=pl.BlockSpec((1, tq, 256), lambda b, t: (b, t, 0)),
            out_shape=jax.ShapeDtypeStruct((B, n_q * tq, 256), BF16),
            compiler_params=_cparams(("parallel", "parallel")),
            name="mla_attention" if latent else "mla_context",
        )(qm, km, vm)

    out = call(MLA_Q_TILE, S // MLA_Q_TILE, 0, True)
    if with_context:
        out = jnp.concatenate([out, call(TM, 1, S // TM, False)], axis=1)
    return out


CH = GDN_CHUNK
NCH = TM // CH


def _split3(a):
    hi = a.astype(BF16)
    r = a - hi.astype(F32)
    mid = r.astype(BF16)
    return hi, mid, (r - mid.astype(F32)).astype(BF16)


def _bmm3(a, b):
    ah, al, _ = _split3(a)
    bh, bl, _ = _split3(b)
    lhs = jnp.concatenate([ah, al, ah], axis=-1)
    rhs = jnp.concatenate([bh, bh, bl], axis=1)
    return jnp.einsum('cij,cjk->cik', lhs, rhs, preferred_element_type=F32)


GDN_HEADS_PER_STEP = 2


def _gdn_local_kernel(x_ref, xp_ref, xn_ref, cw_ref, ba_ref, gp_ref,
                      p1_ref, p2_ref, p3_ref, gt_ref, xs_ref, *, n_lat):
    t = pl.program_id(2)
    has_prev = jnp.logical_and(t > 0, t != n_lat)
    has_next = t < n_lat - 1

    baf = ba_ref[0]
    lane = lax.broadcasted_iota(jnp.int32, (TM, LANE), 1)
    beta_all = jax.nn.sigmoid(baf)
    xg = baf + gp_ref[1:2, :]
    g_all = -jnp.exp(gp_ref[0:1, :]) * (jnp.maximum(xg, 0.0) + jnp.log(1.0 + jnp.exp(-jnp.abs(xg))))

    def col(a, idx):
        cvec = jnp.sum(jnp.where(lane == idx, a, 0.0), axis=1, keepdims=True)
        return jnp.broadcast_to(cvec, (TM, CH)).reshape(NCH, CH, CH)

    ii = lax.broadcasted_iota(jnp.int32, (CH, CH), 0)
    jj = lax.broadcasted_iota(jnp.int32, (CH, CH), 1)
    eye = (ii == jj).astype(F32)[None]
    nt = lambda a, b: jnp.einsum('cid,cjd->cij', a, b, preferred_element_type=F32)

    groups = []
    for hh in range(GDN_HEADS_PER_STEP):
        h = pl.program_id(1) * GDN_HEADS_PER_STEP + hh
        parts = []
        for part in range(3):
            xs_ref[hh, part, 0:8, :] = jnp.where(has_prev, xp_ref[0, part, hh], 0.0)
            xs_ref[hh, part, 8:8 + TM, :] = x_ref[0, part, hh]
            xs_ref[hh, part, 8 + TM:16 + TM, :] = jnp.where(has_next, xn_ref[0, part, hh], 0.0)
            acc = jnp.zeros((TM, GDN_DK), F32)
            for k in range(GDN_CONV):
                acc = acc + cw_ref[hh, part, k:k + 1, :] * xs_ref[hh, part, pl.ds(8 - GDN_CONV // 2 + k, TM), :]
            parts.append(acc * jax.nn.sigmoid(acc))
        q_, k_, v = parts
        q = q_ * lax.rsqrt(jnp.sum(q_ * q_, axis=-1, keepdims=True) + EPS) * (GDN_DK ** -0.5)
        k = k_ * lax.rsqrt(jnp.sum(k_ * k_, axis=-1, keepdims=True) + EPS)
        q3, k3, v3 = (a.reshape(NCH, CH, GDN_DK) for a in (q, k, v))
        kb, qb = k3.astype(BF16), q3.astype(BF16)
        kk, qk = nt(kb, kb), nt(qb, kb)
        for d in range(2):
            groups.append(dict(
                q3=q3, k3=k3, v3=v3, kk=kk, qk=qk,
                beta=col(beta_all, _BA_LANE + GDN_HEADS * d + h),
                g=col(g_all, _BA_LANE + 2 * GDN_HEADS + GDN_HEADS * d + h)))

    ngrp = len(groups)
    stack = lambda key: jnp.concatenate([grp[key] for grp in groups], axis=0)
    per_dir = lambda lo_, up_: jnp.concatenate(
        [jnp.broadcast_to((lo_ if gi % 2 == 0 else up_)[None], (NCH, CH, CH)) for gi in range(ngrp)], axis=0)
    incl = per_dir(jj <= ii, jj >= ii)
    strict = per_dir(jj < ii, jj > ii)
    q3, k3, v3, kk, qk, beta, g = (stack(key) for key in ('q3', 'k3', 'v3', 'kk', 'qk', 'beta', 'g'))
    nb = ngrp * NCH
    tri = incl.astype(BF16)
    gcx = jnp.einsum('cij,cjl->cil', jnp.concatenate([tri] * 3, axis=-1),
                     jnp.concatenate(_split3(g), axis=1), preferred_element_type=F32)
    gcr = jnp.stack([gcx[c].T for c in range(nb)], axis=0)
    decay = jnp.exp(jnp.where(incl, gcx - gcr, -jnp.inf))
    lm = jnp.where(strict, beta * kk * decay, 0.0)
    egc = jnp.exp(gcx)
    rhs = jnp.concatenate([v3 * beta, k3 * beta * egc], axis=-1)
    p = -lm
    tinv = eye + p
    for _ in range(5):
        p = _bmm3(p, p)
        tinv = tinv + _bmm3(tinv, p)
    uw = _bmm3(tinv, rhs)
    gce = jnp.concatenate(
        [gcx[NCH * gi:NCH * (gi + 1), (CH - 1 if gi % 2 == 0 else 0):(CH if gi % 2 == 0 else 1), :] for gi in range(ngrp)],
        axis=0)
    k_out = k3 * jnp.exp(gce - gcx)
    k_out_t = jnp.stack([k_out[c].T for c in range(nb)], axis=0)
    kq = jnp.concatenate([k_out_t, q3 * egc], axis=-1)
    intra = qk * decay
    gte = jnp.exp(gce)
    gtb = jnp.broadcast_to(jnp.concatenate([gte, gte], axis=-1), (nb, 8, LANE))
    for gi in range(ngrp):
        hh, d = gi // 2, gi % 2
        sl = slice(NCH * gi, NCH * (gi + 1))
        p1_ref[d, 0, hh] = uw[sl].reshape(TM, 2 * CH).astype(BF16)
        p2_ref[d, 0, hh] = kq[sl].reshape(TM, 2 * CH).astype(BF16)
        p3_ref[d, 0, hh] = intra[sl].reshape(TM, CH).astype(BF16)
        gt_ref[d, 0, hh] = gtb[sl].reshape(NCH * 8, LANE)


def _gdn_local(gx, conv_w, ba, gp, n_lat):
    B, _, T, _ = gx.shape
    H = GDN_HEADS
    hp = GDN_HEADS_PER_STEP
    nt = T // TM
    gx5 = gx.reshape(B, 3, H, T, GDN_DK)
    cw = jnp.transpose(conv_w.reshape(GDN_CONV, 3, H, GDN_DK), (2, 1, 0, 3))
    kern = functools.partial(_gdn_local_kernel, n_lat=n_lat)
    r8 = TM // 8
    big = lambda w, dt: jax.ShapeDtypeStruct((2, B, H, T, w), dt)
    ospec = lambda w: pl.BlockSpec((2, 1, hp, TM, w), lambda b, h, t: (0, b, h, t, 0))
    return pl.pallas_call(
        kern,
        grid=(B, H // hp, nt),
        in_specs=[pl.BlockSpec((1, 3, hp, TM, GDN_DK), lambda b, h, t: (b, 0, h, t, 0)),
                  pl.BlockSpec((1, 3, hp, 8, GDN_DK), lambda b, h, t: (b, 0, h, jnp.maximum(t * r8 - 1, 0), 0)),
                  pl.BlockSpec((1, 3, hp, 8, GDN_DK), lambda b, h, t: (b, 0, h, jnp.minimum((t + 1) * r8, T // 8 - 1), 0)),
                  pl.BlockSpec((hp, 3, GDN_CONV, GDN_DK), lambda b, h, t: (h, 0, 0, 0)),
                  pl.BlockSpec((1, TM, LANE), lambda b, h, t: (b, t, 0)),
                  pl.BlockSpec((2, LANE), lambda b, h, t: (0, 0))],
        out_specs=[ospec(2 * CH), ospec(2 * CH), ospec(CH),
                   pl.BlockSpec((2, 1, hp, NCH * 8, LANE), lambda b, h, t: (0, b, h, t, 0))],
        out_shape=[big(2 * CH, BF16), big(2 * CH, BF16), big(CH, BF16),
                   jax.ShapeDtypeStruct((2, B, H, (T // CH) * 8, LANE), F32)],
        scratch_shapes=[pltpu.VMEM((hp, 3, TM + 16, GDN_DK), F32)],
        compiler_params=_cparams(("parallel", "parallel", "parallel")),
        name="gdn_local",
    )(gx5, gx5, gx5, cw, ba, gp)


def _gdn_scan_kernel(p1f, p2f, p3f, gtf, p1b, p2b, p3b, gtb, of_ref, ob_ref, s_ref):
    @pl.when(pl.program_id(0) == 0)
    def _init():
        s_ref[...] = jnp.zeros_like(s_ref)

    nb, nh = s_ref.shape[1], s_ref.shape[2]
    n = nb * nh
    bmm = lambda a, b: jnp.einsum('nij,njk->nik', a, b, preferred_element_type=F32)
    new_states = []
    for d, (p1, p2, p3, gt, o_ref) in enumerate(((p1f, p2f, p3f, gtf, of_ref), (p1b, p2b, p3b, gtb, ob_ref))):
        s = s_ref[d].reshape(n, GDN_DK, GDN_DV)
        uw = p1[0].reshape(n, CH, 2 * CH)
        kq = p2[0].reshape(n, CH, 2 * CH)
        m1 = bmm(jnp.concatenate([uw[:, :, CH:], kq[:, :, CH:]], axis=1), s.astype(BF16))
        v_new = (uw[:, :, :CH].astype(F32) - m1[:, :CH]).astype(BF16)
        o = m1[:, CH:] + bmm(p3[0].reshape(n, CH, CH), v_new)
        o_ref[...] = o.reshape(nb, nh, CH, GDN_DV)
        g = gt[0].reshape(n, 8, LANE)[:, 0:1, :CH]
        new_states.append(s * g + bmm(kq[:, :, :CH], v_new))
    for d in range(2):
        s_ref[d] = new_states[d].reshape(nb, nh, GDN_DK, GDN_DV)


def _gdn_scan(p1, p2, p3, gt, S):
    _, B, H, T, _ = p1.shape
    n = T // CH
    n_lat = S // CH
    fwd = lambda i: (n_lat + i) % n
    bwd = lambda i: n - 1 - i
    specs = []
    for d, order in ((0, fwd), (1, bwd)):
        for w, rows in ((2 * CH, CH), (2 * CH, CH), (CH, CH), (LANE, 8)):
            specs.append(pl.BlockSpec((1, B, H, rows, w), functools.partial(lambda i, d, order: (d, 0, 0, order(i), 0), d=d, order=order)))
    out = jax.ShapeDtypeStruct((B, H, T, GDN_DV), F32)
    return pl.pallas_call(
        _gdn_scan_kernel,
        grid=(n,),
        in_specs=specs,
        out_specs=[pl.BlockSpec((B, H, CH, GDN_DV), lambda i: (0, 0, fwd(i), 0)),
                   pl.BlockSpec((B, H, CH, GDN_DV), lambda i: (0, 0, bwd(i), 0))],
        out_shape=[out, out],
        scratch_shapes=[pltpu.VMEM((2, B, H, GDN_DK, GDN_DV), F32)],
        compiler_params=_cparams(("arbitrary",)),
        name="gdn_scan",
    )(p1, p2, p3, gt, p1, p2, p3, gt)


def _rms(v, gain):
    return v * lax.rsqrt(jnp.mean(v * v, axis=-1, keepdims=True) + EPS) * gain


def _mix_out_kernel(x_ref, mod_ref, oa_ref, of_ref, ob_ref, z_ref, oc_ref, gw_ref, wo_ref,
                    post_ref, pre2_ref, rw_ref, rb_ref,
                    x1_ref, h2_ref, wd_ref, pos_ref, rankt_ref, cum_ref, carry_ref, *, tiles_per_super):
    g = pl.program_id(0) * pl.num_programs(1) + pl.program_id(1)

    @pl.when(g % tiles_per_super == 0)
    def _reset():
        carry_ref[...] = jnp.zeros_like(carry_ref)

    z = z_ref[0]
    gated = []
    for h in range(GDN_HEADS):
        o = of_ref[0, h] + ob_ref[0, h]
        zh = z[:, GDN_DV * h:GDN_DV * (h + 1)]
        gated.append(_rms(o, gw_ref[...]) * (zh * jax.nn.sigmoid(zh)))
    mixed = jnp.concatenate([oa_ref[0], jnp.concatenate(gated, axis=1).astype(BF16), oc_ref[0]], axis=1)
    y = jnp.dot(mixed, wo_ref[...], preferred_element_type=F32)
    x1 = x_ref[0] + mod_ref[0, 2:3, :] * _rms(y, post_ref[...])
    x1_ref[0] = x1
    h2 = (_rms(x1, pre2_ref[...]) * (1.0 + mod_ref[0, 4:5, :]) + mod_ref[0, 3:4, :]).astype(BF16)
    h2_ref[...] = h2

    scores = jax.nn.sigmoid(_nt_dot(rw_ref[...], h2))
    sel = scores + rb_ref[...]
    gsz = N_EXPERTS // MOE_GROUPS
    g3 = sel.reshape(MOE_GROUPS, gsz, TM)
    io = lax.broadcasted_iota(jnp.int32, g3.shape, 1)
    m1 = jnp.max(g3, axis=1, keepdims=True)
    i1 = jnp.min(jnp.where(g3 == m1, io, gsz), axis=1, keepdims=True)
    m2 = jnp.max(jnp.where(io == i1, -jnp.inf, g3), axis=1, keepdims=True)

    def top_mask(vals, k):
        n = vals.shape[0]
        idx = lax.broadcasted_iota(jnp.int32, vals.shape, 0)
        mask = jnp.zeros(vals.shape, F32)
        for _ in range(k):
            mx = jnp.max(vals, axis=0, keepdims=True)
            hit = idx == jnp.min(jnp.where(vals == mx, idx, n), axis=0, keepdims=True)
            mask = jnp.where(hit, 1.0, mask)
            vals = jnp.where(hit, -jnp.inf, vals)
        return mask

    gmask = top_mask((m1 + m2).reshape(MOE_GROUPS, TM), MOE_TOPK_GROUPS)
    masked = jnp.where(gmask.reshape(MOE_GROUPS, 1, TM) > 0.0, g3, -jnp.inf).reshape(N_EXPERTS, TM)
    smask = top_mask(masked, MOE_TOP_K)
    w = jnp.where(smask > 0.0, scores, 0.0)
    wn = w / jnp.sum(w, axis=0, keepdims=True) * ROUTED_SCALE
    wd_ref[...] = jnp.where(smask > 0.0, wn, -1.0)

    ci = lax.broadcasted_iota(jnp.int32, (TM, TM), 0)
    cj = lax.broadcasted_iota(jnp.int32, (TM, TM), 1)
    ut = jnp.where(ci <= cj, 1.0, 0.0).astype(BF16)
    cs = jnp.dot(smask.astype(BF16), ut, preferred_element_type=F32)
    carry = carry_ref[...]
    posf = jnp.where(smask > 0.0, carry + cs - 1.0, -1.0)
    pos_ref[...] = posf.astype(jnp.int32)
    rankt_ref[...] = posf.T
    carry = carry + jnp.broadcast_to(cs[:, TM - 1:TM], (N_EXPERTS, TM))
    carry_ref[...] = carry
    cum_ref[0] = carry[:, :LANE]


def _mix_out(xx, modl, oa, o_f, o_b, z, oc, gw, wo, post, pre2, rw, rb, n_lat, nt, tiles_per_super):
    B, T, D = xx.shape
    nb = modl.shape[0] - 1
    N = B * nt * TM

    def mod_map(b, t):
        return (jnp.where(t >= n_lat, nb, b), 0, 0)

    row = lambda w: pl.BlockSpec((1, TM, w), lambda b, t: (b, t, 0))
    flat = lambda b, t: (0, b * nt + t)
    kern = functools.partial(_mix_out_kernel, tiles_per_super=tiles_per_super)
    return pl.pallas_call(
        kern,
        grid=(B, nt),
        in_specs=[row(D), pl.BlockSpec((1, 6, D), mod_map), row(384),
                  pl.BlockSpec((1, GDN_HEADS, TM, GDN_DV), lambda b, t: (b, 0, t, 0)),
                  pl.BlockSpec((1, GDN_HEADS, TM, GDN_DV), lambda b, t: (b, 0, t, 0)),
                  row(384), row(256), _const_spec((1, GDN_DV)), _const_spec(wo.shape),
                  _const_spec((1, D)), _const_spec((1, D)), _const_spec(rw.shape), _const_spec((N_EXPERTS, 1))],
        out_specs=[row(D),
                   pl.BlockSpec((TM, D), lambda b, t: (b * nt + t, 0)),
                   pl.BlockSpec((N_EXPERTS, TM), flat),
                   pl.BlockSpec((N_EXPERTS, TM), flat),
                   pl.BlockSpec((TM, N_EXPERTS), lambda b, t: (b * nt + t, 0)),
                   pl.BlockSpec((1, N_EXPERTS, LANE), lambda b, t: (b * nt + t, 0, 0))],
        out_shape=[jax.ShapeDtypeStruct((B, nt * TM, D), F32),
                   jax.ShapeDtypeStruct((N, D), BF16),
                   jax.ShapeDtypeStruct((N_EXPERTS, N), F32),
                   jax.ShapeDtypeStruct((N_EXPERTS, N), jnp.int32),
                   jax.ShapeDtypeStruct((N, N_EXPERTS), F32),
                   jax.ShapeDtypeStruct((N // TM, N_EXPERTS, LANE), F32)],
        scratch_shapes=[pltpu.VMEM((N_EXPERTS, TM), F32)],
        compiler_params=_cparams(("arbitrary", "arbitrary")),
        name="mix_out_router",
    )(xx, modl, oa, o_f, o_b, z, oc, gw, wo, post, pre2, rw, rb)


MOE_BLOCK = 128


MOE_SUPER_TILES = 3
MOE_CHUNK_BLOCKS = 8
MOE_EXPERTS_PER_STEP = 4
MOE_NO_RANK = 1 << 20


def _moe_max_blocks(tps):
    b = tps * TM * MOE_TOP_K // MOE_BLOCK + N_EXPERTS
    return -(-b // MOE_CHUNK_BLOCKS) * MOE_CHUNK_BLOCKS


def _moe2_meta(cum, n_tiles, tps):
    ns = -(-n_tiles // tps)
    maxblk = _moe_max_blocks(tps)
    last = jnp.minimum((jnp.arange(ns) + 1) * tps - 1, n_tiles - 1)
    cnt = cum[last, :, 0].astype(jnp.int32)
    nslot = (cnt + MOE_BLOCK - 1) // MOE_BLOCK
    bend = jnp.cumsum(nslot, axis=1)
    bstart = bend - nslot
    b = jnp.arange(maxblk, dtype=jnp.int32)
    blk_e = jnp.minimum(jnp.sum(bend[:, None, :] <= b[None, :, None], axis=2), N_EXPERTS - 1)
    blk_k = b[None, :] - jnp.take_along_axis(bstart, blk_e, axis=1)
    blk_r0 = jnp.where(b[None, :] < bend[:, -1:], blk_k * MOE_BLOCK, MOE_NO_RANK)
    i32 = lambda a: a.reshape(-1).astype(jnp.int32)
    return i32(nslot), i32(bstart), i32(bend[:, -1]), i32(blk_e), i32(blk_r0)


def _moe2_kernel(nslot_ref, bstart_ref, nblk_ref, blke_ref, blkr_ref,
                 h2_ref, pos_ref, rankt_ref, wd_ref, wg_ref, wu_ref, wdn_ref, out_ref, yw_ref,
                 *, tps, maxblk, n_tiles):
    s = pl.program_id(0)
    j = pl.program_id(1)
    st = tps * TM

    @pl.when(jnp.logical_and(s == 0, j == 0))
    def _init():
        yw_ref[...] = jnp.zeros_like(yw_ref)

    ne = MOE_EXPERTS_PER_STEP
    n_expert_steps = N_EXPERTS // ne

    @pl.when(j < n_expert_steps)
    def _experts():
        p = s * N_EXPERTS + j * ne
        riota = lax.broadcasted_iota(jnp.int32, (MOE_BLOCK, st), 0)
        nslots = [nslot_ref[p + i] for i in range(ne)]

        def slot(k, c):
            hits = [pos_ref[i] == riota + k * MOE_BLOCK for i in range(ne)]
            onehot = jnp.concatenate([jnp.where(hit, 1.0, 0.0).astype(BF16) for hit in hits], axis=0)
            x = jnp.dot(onehot, h2_ref[...], preferred_element_type=F32).astype(BF16).reshape(ne, MOE_BLOCK, -1)
            bmm = lambda a, w_ref: jnp.einsum('eij,ejk->eik', a, w_ref[...], preferred_element_type=F32)
            hg = bmm(x, wg_ref)
            y = bmm((hg * jax.nn.sigmoid(hg) * bmm(x, wu_ref)).astype(BF16), wdn_ref)
            for i in range(ne):
                @pl.when(k < nslots[i])
                def _store():
                    wcol = jnp.sum(jnp.where(hits[i], wd_ref[i], 0.0), axis=1, keepdims=True)
                    row0 = pl.multiple_of((bstart_ref[p + i] + k) * MOE_BLOCK, MOE_BLOCK)
                    yw_ref[pl.ds(row0, MOE_BLOCK), :] = (y[i] * wcol).astype(BF16)
            return c

        n_slots = nslots[0]
        for i in range(1, ne):
            n_slots = jnp.maximum(n_slots, nslots[i])
        lax.fori_loop(0, n_slots, slot, 0)

    @pl.when(jnp.logical_and(j >= n_expert_steps, s * tps + j - n_expert_steps < n_tiles))
    def _combine():
        t = j - n_expert_steps
        pt = rankt_ref[pl.ds(pl.multiple_of(t * TM, TM), TM), :]
        lane_e = lax.broadcasted_iota(jnp.int32, (TM, N_EXPERTS), 1)
        li = lax.broadcasted_iota(jnp.int32, (TM, MOE_BLOCK), 1)
        rows = MOE_CHUNK_BLOCKS * MOE_BLOCK

        def chunk(c, acc):
            hts = []
            for bi in range(MOE_CHUNK_BLOCKS):
                b = s * maxblk + c * MOE_CHUNK_BLOCKS + bi
                col = jnp.sum(jnp.where(lane_e == blke_ref[b], pt, 0.0), axis=1, keepdims=True)
                hts.append(jnp.where(col == (li + blkr_ref[b]).astype(F32), 1.0, 0.0).astype(BF16))
            ht = jnp.concatenate(hts, axis=1)
            return acc + jnp.dot(ht, yw_ref[pl.ds(pl.multiple_of(c * rows, rows), rows), :],
                                 preferred_element_type=F32)

        n_chunks = (nblk_ref[s] + MOE_CHUNK_BLOCKS - 1) // MOE_CHUNK_BLOCKS
        out_ref[...] = lax.fori_loop(0, n_chunks, chunk, jnp.zeros(out_ref.shape, F32))


def _moe2_routed(h2, wd, pos, rankt, cum, wg, wu, wdn):
    N, D = h2.shape
    E = N_EXPERTS
    n_tiles = N // TM
    tps = min(MOE_SUPER_TILES, n_tiles)
    ns = -(-n_tiles // tps)
    st = tps * TM
    pad = ns * st - N
    if pad:
        h2 = jnp.pad(h2, ((0, pad), (0, 0)))
        wd = jnp.pad(wd, ((0, 0), (0, pad)))
        pos = jnp.pad(pos, ((0, 0), (0, pad)), constant_values=-1)
        rankt = jnp.pad(rankt, ((0, pad), (0, 0)), constant_values=-1.0)
    maxblk = _moe_max_blocks(tps)
    meta = _moe2_meta(cum, n_tiles, tps)
    ne = MOE_EXPERTS_PER_STEP
    nq = E // ne
    ex = lambda j: jnp.minimum(j, nq - 1)
    grid_spec = pltpu.PrefetchScalarGridSpec(
        num_scalar_prefetch=5,
        grid=(ns, nq + tps),
        in_specs=[pl.BlockSpec((st, D), lambda s, j, *_: (s, 0)),
                  pl.BlockSpec((ne, 1, st), lambda s, j, *_: (ex(j), 0, s)),
                  pl.BlockSpec((st, E), lambda s, j, *_: (s, 0)),
                  pl.BlockSpec((ne, 1, st), lambda s, j, *_: (ex(j), 0, s)),
                  pl.BlockSpec((ne, D, D_EXPERT), lambda s, j, *_: (ex(j), 0, 0)),
                  pl.BlockSpec((ne, D, D_EXPERT), lambda s, j, *_: (ex(j), 0, 0)),
                  pl.BlockSpec((ne, D_EXPERT, D), lambda s, j, *_: (ex(j), 0, 0))],
        out_specs=pl.BlockSpec((TM, D), lambda s, j, *_: (jnp.minimum(s * tps + jnp.maximum(j - nq, 0), n_tiles - 1), 0)),
        scratch_shapes=[pltpu.VMEM((maxblk * MOE_BLOCK, D), BF16)],
    )
    return pl.pallas_call(
        functools.partial(_moe2_kernel, tps=tps, maxblk=maxblk, n_tiles=n_tiles),
        grid_spec=grid_spec,
        out_shape=jax.ShapeDtypeStruct((N, D), F32),
        compiler_params=_cparams(("arbitrary", "arbitrary")),
        name="moe_routed",
    )(*meta, h2, pos.reshape(E, 1, ns * st), rankt, wd.reshape(E, 1, ns * st), wg, wu, wdn)


def _ffn_out_kernel(x1_ref, mod_ref, h2_ref, routed_ref, sg_ref, su_ref, sd_ref, post_ref, o_ref):
    h2 = h2_ref[...]
    hg = jnp.dot(h2, sg_ref[...], preferred_element_type=F32)
    hu = jnp.dot(h2, su_ref[...], preferred_element_type=F32)
    f = jnp.dot((hg * jax.nn.sigmoid(hg) * hu).astype(BF16), sd_ref[...], preferred_element_type=F32) + routed_ref[...]
    o_ref[0] = x1_ref[0] + mod_ref[0, 5:6, :] * _rms(f, post_ref[...])


def _ffn_out(x1, modl, h2, routed, sg, su, sd, post, n_lat):
    B, Tn, D = x1.shape
    nt = Tn // TM
    nb = modl.shape[0] - 1

    def mod_map(b, t):
        return (jnp.where(t >= n_lat, nb, b), 0, 0)

    row = pl.BlockSpec((1, TM, D), lambda b, t: (b, t, 0))
    flat = pl.BlockSpec((TM, D), lambda b, t: (b * nt + t, 0))
    return pl.pallas_call(
        _ffn_out_kernel,
        grid=(B, nt),
        in_specs=[row, pl.BlockSpec((1, 6, D), mod_map), flat, flat,
                  _const_spec(sg.shape), _const_spec(su.shape), _const_spec(sd.shape), _const_spec((1, D))],
        out_specs=row,
        out_shape=jax.ShapeDtypeStruct((B, Tn, D), F32),
        compiler_params=_cparams(("parallel", "parallel")),
        name="ffn_out",
    )(x1, modl, h2, routed, sg, su, sd, post)


def kernel(x, c, ctx, c_ctx, ada_w, ada_b, mix_norm_pre, mix_norm_post, ffn_norm_pre, ffn_norm_post, w_in, w_out, swa_sink, gdn_conv_w, gdn_a_log, gdn_dt_bias, gdn_norm_w, mla_q_norm, mla_w_uq, mla_kv_norm, mla_w_ukv, router_w, router_bias, expert_w_gate, expert_w_up, expert_w_down, shared_w_gate, shared_w_up, shared_w_down):
    B, S, D = x.shape
    L = ctx.shape[1]
    assert L == TM and S % (2 * TM) == 0
    T = S + L
    n_lat = S // TM
    xx = jnp.concatenate([x, ctx], axis=1)
    tabs = _rope_tables(S, T)
    cvecs = jnp.concatenate([c, c_ctx[None], jnp.zeros((8 - B - 1, D), F32)], axis=0)
    mods = _modulation(cvecs, ada_w, ada_b).reshape(DEPTH, 8, 6, D)[:, :B + 1]
    for layer in range(DEPTH):
        xx = _layer(layer, xx, mods[layer], tabs, S, layer == DEPTH - 1,
                    mix_norm_pre, mix_norm_post, ffn_norm_pre, ffn_norm_post, w_in, w_out, swa_sink, gdn_conv_w,
                    gdn_a_log, gdn_dt_bias, gdn_norm_w, mla_q_norm, mla_w_uq, mla_kv_norm, mla_w_ukv, router_w,
                    router_bias, expert_w_gate, expert_w_up, expert_w_down, shared_w_gate, shared_w_up, shared_w_down)
    return xx


def _layer(layer, xx, modl, tabs, S, last, mix_norm_pre, mix_norm_post, ffn_norm_pre, ffn_norm_post, w_in, w_out,
           swa_sink, gdn_conv_w, gdn_a_log, gdn_dt_bias, gdn_norm_w, mla_q_norm, mla_w_uq, mla_kv_norm, mla_w_ukv,
           router_w, router_bias, expert_w_gate, expert_w_up, expert_w_down, shared_w_gate, shared_w_up, shared_w_down):
    B, T, D = xx.shape
    n_lat = S // TM
    nt = n_lat if last else T // TM
    qa, ka, va, gx, z, ba, qm, km, vm = _project(
        xx, modl, mix_norm_pre[layer][None], _prep_w_in(w_in[layer]), _prep_w_uq(mla_w_uq[layer]),
        _prep_w_ukv(mla_w_ukv[layer]), jnp.pad(mla_q_norm[layer], (0, 256 - MLA_Q_RANK))[None],
        mla_kv_norm[layer][None], tabs, n_lat)
    oa = _swa_attention(swa_sink[layer], qa, ka, va, S, nt)
    oc = _mla_attention(qm, km, vm, S, not last)
    gp = jnp.zeros((2, LANE), F32)
    g0 = _BA_LANE + 2 * GDN_HEADS
    gp = gp.at[0, g0:g0 + 2 * GDN_HEADS].set(gdn_a_log[layer].reshape(-1))
    gp = gp.at[1, g0:g0 + 2 * GDN_HEADS].set(gdn_dt_bias[layer].reshape(-1))
    p1, p2, p3, gt = _gdn_local(gx, gdn_conv_w[layer], ba, gp, n_lat)
    o_f, o_b = _gdn_scan(p1, p2, p3, gt, S)

    x1, h2, wd, pos, rankt, cum = _mix_out(
        xx, modl, oa, o_f, o_b, z, oc, gdn_norm_w[layer][None], _prep_w_out(w_out[layer]),
        mix_norm_post[layer][None], ffn_norm_pre[layer][None], router_w[layer].T.astype(BF16),
        router_bias[layer][:, None], n_lat, nt, min(MOE_SUPER_TILES, B * nt))
    routed = _moe2_routed(h2, wd, pos, rankt, cum, expert_w_gate[layer].astype(BF16),
                          expert_w_up[layer].astype(BF16), expert_w_down[layer].astype(BF16))
    return _ffn_out(x1, modl, h2, routed, shared_w_gate[layer].astype(BF16), shared_w_up[layer].astype(BF16),
                    shared_w_down[layer].astype(BF16), ffn_norm_post[layer][None], n_lat)
```

```python
import functools
import math

import numpy as np
import jax
import jax.numpy as jnp
from jax import lax
from jax.experimental import pallas as pl
from jax.experimental.pallas import tpu as pltpu

F32 = jnp.float32
BF16 = jnp.bfloat16

DEPTH = 2
GRID_W = 64
EPS = 1e-6
ROPE_BASE = 10000.0
HEAD_DIM = 64
SWA_HEADS = 6
SWA_KV_HEADS = 2
SWA_WINDOW = 128
GDN_HEADS = 6
GDN_DK = 64
GDN_DV = 64
GDN_CONV = 5
GDN_CHUNK = 64
MLA_HEADS = 4
MLA_Q_RANK = 192
MLA_KV_RANK = 128
MLA_NOPE = 64
MLA_ROPE = 32
MLA_V = 64
N_EXPERTS = 64
MOE_TOP_K = 8
MOE_GROUPS = 8
MOE_TOPK_GROUPS = 4
D_EXPERT = 256
ROUTED_SCALE = 2.5
LOG2_E = math.log2(math.e)

_SPLITS = (384, 128, 128, 1152, 384, 24, 192, 128, 32)
_OFF = np.concatenate([[0], np.cumsum(_SPLITS)]).tolist()
D_PROJ = _OFF[-1]

TM = 256
LANE = 128
VMEM_LIMIT = 56 * 1024 * 1024

_C_QA, _C_QAS, _C_KA, _C_KAS, _C_VA, _C_GDN, _C_Z, _C_CQ, _C_CKV, _C_KRB, _C_END = (
    0, 768, 1536, 1664, 1792, 1920, 3072, 3456, 3712, 3840, 3968)
_BA_LANE = 32


def _cparams(sem, vmem=VMEM_LIMIT):
    return pltpu.CompilerParams(dimension_semantics=sem, vmem_limit_bytes=vmem)


def _rope_partner(d, width):
    half, n = width // 2, width // 4
    i = d % half
    return (d // half) * half + (i + n if i < n else i - n)


def _take_cols(w, idx):
    idx = [int(i) for i in idx]
    pieces, start = [], 0
    for pos in range(1, len(idx) + 1):
        run_ends = pos == len(idx) or (idx[pos] != idx[pos - 1] + 1 if idx[pos - 1] >= 0 else idx[pos] >= 0) \
            or (idx[pos] < 0) != (idx[pos - 1] < 0)
        if run_ends:
            n = pos - start
            pieces.append(jnp.zeros((w.shape[0], n), w.dtype) if idx[start] < 0 else w[:, idx[start]:idx[start] + n])
            start = pos
    return jnp.concatenate(pieces, axis=1)


def _prep_w_in(w_in):
    cols = []
    for swap in (False, True):
        for h in range(SWA_HEADS):
            j = h // (SWA_HEADS // SWA_KV_HEADS)
            blk = [-1] * LANE
            for d in range(HEAD_DIM):
                blk[64 * j + d] = _OFF[0] + h * HEAD_DIM + (_rope_partner(d, HEAD_DIM) if swap else d)
            cols += blk
    for swap in (False, True):
        for j in range(SWA_KV_HEADS):
            cols += [_OFF[1] + j * HEAD_DIM + (_rope_partner(d, HEAD_DIM) if swap else d) for d in range(HEAD_DIM)]
    cols += list(range(_OFF[2], _OFF[3]))
    cols += list(range(_OFF[3], _OFF[4]))
    cols += list(range(_OFF[4], _OFF[5]))
    cols += list(range(_OFF[6], _OFF[7])) + [-1] * 64
    cols += list(range(_OFF[7], _OFF[8]))
    cols += list(range(_OFF[8], _OFF[9])) + list(range(_OFF[5], _OFF[6])) + [-1] * (LANE - 32 - 24)
    assert len(cols) == _C_END
    return _take_cols(w_in, cols).astype(BF16)


def _prep_w_uq(w_uq):
    cols = []
    for swap in (False, True):
        for h in range(MLA_HEADS):
            base = h * (MLA_NOPE + MLA_ROPE)
            blk = [-1] * LANE
            for d in range(MLA_NOPE):
                blk[d] = -1 if swap else base + d
            for r in range(MLA_ROPE):
                blk[MLA_NOPE + r] = base + MLA_NOPE + (_rope_partner(r, MLA_ROPE) if swap else r)
            cols += blk
    w = _take_cols(w_uq, cols)
    return jnp.pad(w, ((0, 256 - MLA_Q_RANK), (0, 0))).astype(BF16)


def _prep_w_ukv(w_ukv):
    kcols, vcols = [], []
    for h in range(MLA_HEADS):
        base = h * (MLA_NOPE + MLA_V)
        kcols += [base + d for d in range(MLA_NOPE)] + [-1] * 64
        vcols += [base + MLA_NOPE + d for d in range(MLA_V)]
    wk = _take_cols(w_ukv, kcols)
    wv = _take_cols(w_ukv, vcols)
    top = jnp.concatenate([wk, jnp.zeros_like(wk), wv], axis=1)
    place = np.zeros((128, 1280), np.float32)
    for h in range(MLA_HEADS):
        for r in range(MLA_ROPE):
            place[r, 128 * h + MLA_NOPE + r] = 1.0
            place[_rope_partner(r, MLA_ROPE), 512 + 128 * h + MLA_NOPE + r] = 1.0
    return jnp.concatenate([top, jnp.asarray(place)], axis=0).astype(BF16)


def _prep_w_out(w_out):
    rows = []
    G = SWA_HEADS // SWA_KV_HEADS
    for g in range(G):
        for j in range(SWA_KV_HEADS):
            rows += [(G * j + g) * HEAD_DIM + d for d in range(HEAD_DIM)]
    rows += list(range(SWA_HEADS * HEAD_DIM, w_out.shape[0]))
    return jnp.take(w_out, jnp.asarray(rows), axis=0).astype(BF16)


def _rope_tables(S, T):
    t = np.arange(S)
    row, col = t // GRID_W, t % GRID_W

    def tab(width, lanes):
        half, n = width // 2, width // 4
        c = np.ones((T, LANE), np.float64)
        s = np.zeros((T, LANE), np.float64)
        for lane, d in lanes:
            i = d % half
            pos = row if d < half else col
            ang =(pos.astype(np.float32) * np.float32(ROPE_BASE ** (-(i % n) / n))).astype(np.float64)
            c[:S, lane] = np.cos(ang)
            s[:S, lane] = -np.sin(ang) if i < n else np.sin(ang)
        return jnp.asarray(c, F32), jnp.asarray(s, F32)

    ca, sa = tab(HEAD_DIM, [(l, l % HEAD_DIM) for l in range(LANE)])
    cm, sm = tab(MLA_ROPE, [(MLA_NOPE + r, r) for r in range(MLA_ROPE)])
    return ca, sa, cm, sm


def _mod_kernel(c_ref, w_ref, b_ref, o_ref):
    cv = c_ref[...]
    a = (cv * jax.nn.sigmoid(cv)).astype(BF16)
    o_ref[0] = jnp.dot(a, w_ref[0].astype(BF16), preferred_element_type=F32) + b_ref[0]


def _modulation(cvecs, ada_w, ada_b):
    depth, D, N = ada_w.shape
    tn = 512
    return pl.pallas_call(
        _mod_kernel,
        grid=(depth, N // tn),
        in_specs=[pl.BlockSpec((8, D), lambda l, j: (0, 0)),
                  pl.BlockSpec((1, D, tn), lambda l, j: (l, 0, j)),
                  pl.BlockSpec((1, 1, tn), lambda l, j: (l, 0, j))],
        out_specs=pl.BlockSpec((1, 8, tn), lambda l, j: (l, 0, j)),
        out_shape=jax.ShapeDtypeStruct((depth, 8, N), F32),
        compiler_params=_cparams(("arbitrary", "arbitrary")),
        name="modulation",
    )(cvecs, ada_w, ada_b.reshape(depth, 1, N))


def _proj_kernel(x_ref, mod_ref, gain_ref, w_ref, wq2_ref, wk2_ref, qg_ref, kvg_ref,
                 ca_ref, sa_ref, cm_ref, sm_ref,
                 qa_ref, ka_ref, va_ref, gx_ref, z_ref, ba_ref, qm_ref, km_ref, vm_ref):
    x = x_ref[0]
    ms = jnp.mean(x * x, axis=-1, keepdims=True)
    h = x * lax.rsqrt(ms + EPS) * gain_ref[...]
    h = h * (1.0 + mod_ref[0, 1:2, :]) + mod_ref[0, 0:1, :]
    hb = h.astype(BF16)

    ca, sa, cm, sm = ca_ref[...], sa_ref[...], cm_ref[...], sm_ref[...]

    def rope(a, b, c, s):
        n = a.shape[1] // LANE
        return a * jnp.concatenate([c] * n, axis=1) + b * jnp.concatenate([s] * n, axis=1)

    p1 = jnp.dot(hb, w_ref[:, _C_QA:_C_GDN], preferred_element_type=F32)
    qa = rope(p1[:, _C_QA:_C_QAS], p1[:, _C_QAS:_C_KA], ca, sa)
    qa_ref[0] = (qa * (HEAD_DIM ** -0.5)).astype(BF16)
    ka_ref[0] = rope(p1[:, _C_KA:_C_KAS], p1[:, _C_KAS:_C_VA], ca, sa).astype(BF16)
    va_ref[0] = p1[:, _C_VA:_C_GDN].astype(BF16)

    p2 = jnp.dot(hb, w_ref[:, _C_GDN:_C_Z], preferred_element_type=F32)
    for j in range(3 * GDN_HEADS):
        gx_ref[0, j] = p2[:, 64 * j:64 * j + 64]

    p3 = jnp.dot(hb, w_ref[:, _C_Z:_C_END], preferred_element_type=F32)
    z_ref[0] = p3[:, 0:_C_CQ - _C_Z]
    cq = p3[:, _C_CQ - _C_Z:_C_CKV - _C_Z]
    ckv = p3[:, _C_CKV - _C_Z:_C_KRB - _C_Z]
    krb = p3[:, _C_KRB - _C_Z:]
    ba_ref[0] = krb

    cqn = cq * lax.rsqrt(jnp.sum(cq * cq, axis=-1, keepdims=True) * (1.0 / MLA_Q_RANK) + EPS) * qg_ref[...]
    e = jnp.dot(cqn.astype(BF16), wq2_ref[...], preferred_element_type=F32)
    qm = rope(e[:, :512], e[:, 512:], cm, sm)
    qm_ref[0] = (qm * (LOG2_E * (MLA_NOPE + MLA_ROPE) ** -0.5)).astype(BF16)

    ckvn = ckv * lax.rsqrt(jnp.mean(ckv * ckv, axis=-1, keepdims=True) + EPS) * kvg_ref[...]
    lhs2 = jnp.concatenate([ckvn.astype(BF16), krb.astype(BF16)], axis=1)
    e2 = jnp.dot(lhs2, wk2_ref[...], preferred_element_type=F32)
    km_ref[0] = rope(e2[:, :512], e2[:, 512:1024], cm, sm).astype(BF16)
    vm_ref[0] = e2[:, 1024:].astype(BF16)


def _const_spec(shape):
    nd = len(shape)
    return pl.BlockSpec(shape, lambda *_: (0,) * nd)


def _project(xx, modl, gain, w_main, wq2, wk2, qg, kvg, tabs, n_lat_tiles):
    B, T, D = xx.shape
    nt = T // TM
    nb = modl.shape[0] - 1

    def mod_map(b, t):
        return (jnp.where(t >= n_lat_tiles, nb, b), 0, 0)

    row = lambda w: pl.BlockSpec((1, TM, w), lambda b, t: (b, t, 0))
    tab = pl.BlockSpec((TM, LANE), lambda b, t: (t, 0))
    out_shapes = [
        jax.ShapeDtypeStruct((B, T, 768), BF16),
        jax.ShapeDtypeStruct((B, T, 128), BF16),
        jax.ShapeDtypeStruct((B, T, 128), BF16),
        jax.ShapeDtypeStruct((B, 18, T, 64), F32),
        jax.ShapeDtypeStruct((B, T, 384), F32),
        jax.ShapeDtypeStruct((B, T, 128), F32),
        jax.ShapeDtypeStruct((B, T, 512), BF16),
        jax.ShapeDtypeStruct((B, T, 512), BF16),
        jax.ShapeDtypeStruct((B, T, 256), BF16),
    ]
    out_specs = [row(768), row(128), row(128),
                 pl.BlockSpec((1, 18, TM, 64), lambda b, t: (b, 0, t, 0)),
                 row(384), row(128), row(512), row(512), row(256)]
    return pl.pallas_call(
        _proj_kernel,
        grid=(B, nt),
        in_specs=[row(D), pl.BlockSpec((1, 6, D), mod_map), _const_spec((1, D)),
                  _const_spec(w_main.shape), _const_spec(wq2.shape), _const_spec(wk2.shape),
                  _const_spec((1, 256)), _const_spec((1, 128)), tab, tab, tab, tab],
        out_specs=out_specs,
        out_shape=out_shapes,
        compiler_params=_cparams(("parallel", "parallel")),
        name="in_proj",
    )(xx, modl, gain, w_main, wq2, wk2, qg, kvg, *tabs)


def _nt_dot(a, b):
    return lax.dot_general(a, b, (((1,), (1,)), ((), ())), preferred_element_type=F32)


def _swa_kernel(sink_ref, q_ref, k_ref, v_ref, o_ref, *, S, n_lat):
    i = pl.program_id(1)
    G = SWA_HEADS // SWA_KV_HEADS
    W = 2 * TM
    lane = lax.broadcasted_iota(jnp.int32, (TM, LANE), 1)
    kc = k_ref[0, pl.ds(S, TM), :]
    vc = v_ref[0, pl.ds(S, TM), :]

    def heads(local):
        outs = []
        for g in range(G):
            og = []
            for j in range(SWA_KV_HEADS):
                h = G * j + g
                q = q_ref[0, :, LANE * h:LANE * (h + 1)]
                sink = sink_ref[h]
                s_ctx = _nt_dot(q, kc)
                m = jnp.maximum(jnp.max(s_ctx, axis=-1, keepdims=True), sink)
                if local is not None:
                    kw, vw, valid = local
                    s_loc = jnp.where(valid, _nt_dot(q, kw), -jnp.inf)
                    m = jnp.maximum(m, jnp.max(s_loc, axis=-1, keepdims=True))
                p_ctx = jnp.exp(s_ctx - m)
                den = jnp.sum(p_ctx, axis=-1, keepdims=True) + jnp.exp(sink - m)
                o = jnp.dot(p_ctx.astype(BF16), vc, preferred_element_type=F32)
                if local is not None:
                    p_loc = jnp.exp(s_loc - m)
                    den = den + jnp.sum(p_loc, axis=-1, keepdims=True)
                    o = o + jnp.dot(p_loc.astype(BF16), vw, preferred_element_type=F32)
                og.append(o * (1.0 / den))
            outs.append(jnp.where(lane < HEAD_DIM, og[0], og[1]))
        o_ref[0] = jnp.concatenate(outs, axis=1).astype(BF16)

    @pl.when(i < n_lat)
    def _latent():
        start = pl.multiple_of(jnp.clip(i * TM - SWA_WINDOW, 0, S - W), LANE)
        kw = k_ref[0, pl.ds(start, W), :]
        vw = v_ref[0, pl.ds(start, W), :]
        qpos = i * TM + lax.broadcasted_iota(jnp.int32, (TM, W), 0)
        kpos = start + lax.broadcasted_iota(jnp.int32, (TM, W), 1)
        heads((kw, vw, jnp.abs(qpos - kpos) <= SWA_WINDOW))

    @pl.when(i >= n_lat)
    def _context():
        heads(None)


def _swa_attention(sink, qa, ka, va, S, nt):
    B, T, _ = qa.shape
    kern = functools.partial(_swa_kernel, S=S, n_lat=S // TM)
    return pl.pallas_call(
        kern,
        grid=(B, nt),
        in_specs=[pl.BlockSpec(memory_space=pltpu.SMEM),
                  pl.BlockSpec((1, TM, 768), lambda b, t: (b, t, 0)),
                  pl.BlockSpec((1, T, 128), lambda b, t: (b, 0, 0)),
                  pl.BlockSpec((1, T, 128), lambda b, t: (b, 0, 0))],
        out_specs=pl.BlockSpec((1, TM, 384), lambda b, t: (b, t, 0)),
        out_shape=jax.ShapeDtypeStruct((B, nt * TM, 384), BF16),
        compiler_params=_cparams(("parallel", "parallel")),
        name="swa_attention",
    )(sink, qa, ka, va)


MLA_SUBSTEPS = 2
MLA_KEY_CHUNK = 4096


def _mla_kernel(q_ref, k_ref, v_ref, o_ref, *, S, tq, latent):
    lane = lax.broadcasted_iota(jnp.int32, (tq, LANE), 1)
    chunk = max(c for c in (MLA_KEY_CHUNK, 1024, 512) if S % c == 0)
    n_chunks = S // chunk if latent else 0
    sub = chunk // MLA_SUBSTEPS
    ctx_keys = TM

    outs = []
    for h in range(MLA_HEADS):
        q = q_ref[0, :, LANE * h:LANE * (h + 1)]

        def step(off, size, carry):
            m, l, acc = carry
            s = _nt_dot(q, k_ref[0, pl.ds(off, size), LANE * h:LANE * (h + 1)])
            blocks = [s[:, LANE * j:LANE * (j + 1)] for j in range(size // LANE)]
            mx = blocks[0]
            for blk in blocks[1:]:
                mx = jnp.maximum(mx, blk)
            m_new = jnp.maximum(m, jnp.broadcast_to(jnp.max(mx, axis=-1, keepdims=True), (tq, LANE)))
            alpha = jnp.exp2(m - m_new)
            ps = [jnp.exp2(blk - m_new) for blk in blocks]
            l = alpha * l
            for p in ps:
                l = l + p
            p = jnp.concatenate(ps, axis=1).astype(BF16)
            v = v_ref[0, pl.ds(off, size), LANE * (h // 2):LANE * (h // 2 + 1)]
            return m_new, l, alpha * acc + jnp.dot(p, v, preferred_element_type=F32)

        def body(c, cr):
            off = pl.multiple_of(c * chunk, chunk)
            for u in range(MLA_SUBSTEPS):
                cr = step(pl.multiple_of(off + u * sub, TM), sub, cr)
            return cr

        zero = jnp.zeros((tq, LANE), F32)
        carry = (jnp.full((tq, LANE), -jnp.inf, F32), zero, zero)
        if n_chunks:
            carry = lax.fori_loop(0, n_chunks, body, carry)
        m, l, acc = step(S, ctx_keys, carry)
        outs.append(acc * (1.0 / jnp.sum(l, axis=-1, keepdims=True)))
    o_ref[0] = jnp.concatenate([jnp.where(lane < MLA_V, outs[0], outs[1]),
                                jnp.where(lane < MLA_V, outs[2], outs[3])], axis=1).astype(BF16)


MLA_Q_TILE = 256


def _mla_attention(qm, km, vm, S, with_context):
    B, T, _ = qm.shape

    def call(tq, n_q, q_block0, latent):
        return pl.pallas_call(
            functools.partial(_mla_kernel, S=S, tq=tq, latent=latent),
            grid=(B, n_q),
            in_specs=[pl.BlockSpec((1, tq, 512), lambda b, t: (b, q_block0 + t, 0)),
                      pl.BlockSpec((1, T, 512), lambda b, t: (b, 0, 0)),
                      pl.BlockSpec((1, T, 256), lambda b, t: (b, 0, 0))],
            out_specs=pl.BlockSpec((1, tq, 256), lambda b, t: (b, t, 0)),
            out_shape=jax.ShapeDtypeStruct((B, n_q * tq, 256), BF16),
            compiler_params=_cparams(("parallel", "parallel")),
            name="mla_attention" if latent else "mla_context",
        )(qm, km, vm)

    out = call(MLA_Q_TILE, S // MLA_Q_TILE, 0, True)
    if with_context:
        out = jnp.concatenate([out, call(TM, 1, S // TM, False)], axis=1)
    return out


CH = GDN_CHUNK
NCH = TM // CH


def _split3(a):
    hi = a.astype(BF16)
    r = a - hi.astype(F32)
    mid = r.astype(BF16)
    return hi, mid, (r - mid.astype(F32)).astype(BF16)


def _bmm3(a, b):
    ah, al, _ = _split3(a)
    bh, bl, _ = _split3(b)
    lhs = jnp.concatenate([ah, al, ah], axis=-1)
    rhs = jnp.concatenate([bh, bh, bl], axis=1)
    return jnp.einsum('cij,cjk->cik', lhs, rhs, preferred_element_type=F32)


GDN_HEADS_PER_STEP = 3


def _gdn_local_kernel(x_ref, xp_ref, xn_ref, cw_ref, ba_ref, gp_ref,
                      p1_ref, p2_ref, p3_ref, gt_ref, xs_ref, *, n_lat):
    t = pl.program_id(2)
    has_prev = jnp.logical_and(t > 0, t != n_lat)
    has_next = t < n_lat - 1

    baf = ba_ref[0]
    lane = lax.broadcasted_iota(jnp.int32, (TM, LANE), 1)
    beta_all = jax.nn.sigmoid(baf)
    xg = baf + gp_ref[1:2, :]
    g_all = -jnp.exp(gp_ref[0:1, :]) * (jnp.maximum(xg, 0.0) + jnp.log(1.0 + jnp.exp(-jnp.abs(xg))))

    def col(a, idx):
        cvec = jnp.sum(jnp.where(lane == idx, a, 0.0), axis=1, keepdims=True)
        return jnp.broadcast_to(cvec, (TM, CH)).reshape(NCH, CH, CH)

    ii = lax.broadcasted_iota(jnp.int32, (CH, CH), 0)
    jj = lax.broadcasted_iota(jnp.int32, (CH, CH), 1)
    eye = (ii == jj).astype(F32)[None]
    nt = lambda a, b: jnp.einsum('cid,cjd->cij', a, b, preferred_element_type=F32)

    groups = []
    for hh in range(GDN_HEADS_PER_STEP):
        h = pl.program_id(1) * GDN_HEADS_PER_STEP + hh
        parts = []
        for part in range(3):
            xs_ref[hh, part, 0:8, :] = jnp.where(has_prev, xp_ref[0, part, hh], 0.0)
            xs_ref[hh, part, 8:8 + TM, :] = x_ref[0, part, hh]
            xs_ref[hh, part, 8 + TM:16 + TM, :] = jnp.where(has_next, xn_ref[0, part, hh], 0.0)
            acc = jnp.zeros((TM, GDN_DK), F32)
            for k in range(GDN_CONV):
                acc = acc + cw_ref[hh, part, k:k + 1, :] * xs_ref[hh, part, pl.ds(8 - GDN_CONV // 2 + k, TM), :]
            parts.append(acc * jax.nn.sigmoid(acc))
        q_, k_, v = parts
        q = q_ * lax.rsqrt(jnp.sum(q_ * q_, axis=-1, keepdims=True) + EPS) * (GDN_DK ** -0.5)
        k = k_ * lax.rsqrt(jnp.sum(k_ * k_, axis=-1, keepdims=True) + EPS)
        q3, k3, v3 = (a.reshape(NCH, CH, GDN_DK) for a in (q, k, v))
        kb, qb = k3.astype(BF16), q3.astype(BF16)
        kk, qk = nt(kb, kb), nt(qb, kb)
        for d in range(2):
            groups.append(dict(
                q3=q3, k3=k3, v3=v3, kk=kk, qk=qk,
                beta=col(beta_all, _BA_LANE + GDN_HEADS * d + h),
                g=col(g_all, _BA_LANE + 2 * GDN_HEADS + GDN_HEADS * d + h)))

    ngrp = len(groups)
    stack = lambda key: jnp.concatenate([grp[key] for grp in groups], axis=0)
    per_dir = lambda lo_, up_: jnp.concatenate(
        [jnp.broadcast_to((lo_ if gi % 2 == 0 else up_)[None], (NCH, CH, CH)) for gi in range(ngrp)], axis=0)
    incl = per_dir(jj <= ii, jj >= ii)
    strict = per_dir(jj < ii, jj > ii)
    q3, k3, v3, kk, qk, beta, g = (stack(key) for key in ('q3', 'k3', 'v3', 'kk', 'qk', 'beta', 'g'))
    nb = ngrp * NCH
    tri = incl.astype(BF16)
    gcx = jnp.einsum('cij,cjl->cil', jnp.concatenate([tri] * 3, axis=-1),
                     jnp.concatenate(_split3(g), axis=1), preferred_element_type=F32)
    gcr = jnp.stack([gcx[c].T for c in range(nb)], axis=0)
    decay = jnp.exp(jnp.where(incl, gcx - gcr, -jnp.inf))
    lm = jnp.where(strict, beta * kk * decay, 0.0)
    egc = jnp.exp(gcx)
    rhs = jnp.concatenate([v3 * beta, k3 * beta * egc], axis=-1)
    p = -lm
    tinv = eye + p
    for _ in range(5):
        p = _bmm3(p, p)
        tinv = tinv + _bmm3(tinv, p)
    uw = _bmm3(tinv, rhs)
    gce = jnp.concatenate(
        [gcx[NCH * gi:NCH * (gi + 1), (CH - 1 if gi % 2 == 0 else 0):(CH if gi % 2 == 0 else 1), :] for gi in range(ngrp)],
        axis=0)
    k_out = k3 * jnp.exp(gce - gcx)
    k_out_t = jnp.stack([k_out[c].T for c in range(nb)], axis=0)
    kq = jnp.concatenate([k_out_t, q3 * egc], axis=-1)
    intra = qk * decay
    gte = jnp.exp(gce)
    gtb = jnp.broadcast_to(jnp.concatenate([gte, gte], axis=-1), (nb, 8, LANE))
    for gi in range(ngrp):
        hh, d = gi // 2, gi % 2
        sl = slice(NCH * gi, NCH * (gi + 1))
        p1_ref[d, 0, hh] = uw[sl].reshape(TM, 2 * CH).astype(BF16)
        p2_ref[d, 0, hh] = kq[sl].reshape(TM, 2 * CH).astype(BF16)
        p3_ref[d, 0, hh] = intra[sl].reshape(TM, CH).astype(BF16)
        gt_ref[d, 0, hh] = gtb[sl].reshape(NCH * 8, LANE)


def _gdn_local(gx, conv_w, ba, gp, n_lat):
    B, _, T, _ = gx.shape
    H = GDN_HEADS
    hp = GDN_HEADS_PER_STEP
    nt = T // TM
    gx5 = gx.reshape(B, 3, H, T, GDN_DK)
    cw = jnp.transpose(conv_w.reshape(GDN_CONV, 3, H, GDN_DK), (2, 1, 0, 3))
    kern = functools.partial(_gdn_local_kernel, n_lat=n_lat)
    r8 = TM // 8
    big = lambda w, dt: jax.ShapeDtypeStruct((2, B, H, T, w), dt)
    ospec = lambda w: pl.BlockSpec((2, 1, hp, TM, w), lambda b, h, t: (0, b, h, t, 0))
    return pl.pallas_call(
        kern,
        grid=(B, H // hp, nt),
        in_specs=[pl.BlockSpec((1, 3, hp, TM, GDN_DK), lambda b, h, t: (b, 0, h, t, 0)),
                  pl.BlockSpec((1, 3, hp, 8, GDN_DK), lambda b, h, t: (b, 0, h, jnp.maximum(t * r8 - 1, 0), 0)),
                  pl.BlockSpec((1, 3, hp, 8, GDN_DK), lambda b, h, t: (b, 0, h, jnp.minimum((t + 1) * r8, T // 8 - 1), 0)),
                  pl.BlockSpec((hp, 3, GDN_CONV, GDN_DK), lambda b, h, t: (h, 0, 0, 0)),
                  pl.BlockSpec((1, TM, LANE), lambda b, h, t: (b, t, 0)),
                  pl.BlockSpec((2, LANE), lambda b, h, t: (0, 0))],
        out_specs=[ospec(2 * CH), ospec(2 * CH), ospec(CH),
                   pl.BlockSpec((2, 1, hp, NCH * 8, LANE), lambda b, h, t: (0, b, h, t, 0))],
        out_shape=[big(2 * CH, BF16), big(2 * CH, BF16), big(CH, BF16),
                   jax.ShapeDtypeStruct((2, B, H, (T // CH) * 8, LANE), F32)],
        scratch_shapes=[pltpu.VMEM((hp, 3, TM + 16, GDN_DK), F32)],
        compiler_params=_cparams(("parallel", "parallel", "parallel")),
        name="gdn_local",
    )(gx5, gx5, gx5, cw, ba, gp)


def _gdn_scan_kernel(p1f, p2f, p3f, gtf, p1b, p2b, p3b, gtb, of_ref, ob_ref, s_ref):
    @pl.when(pl.program_id(0) == 0)
    def _init():
        s_ref[...] = jnp.zeros_like(s_ref)

    nb, nh = s_ref.shape[1], s_ref.shape[2]
    n = nb * nh
    bmm = lambda a, b: jnp.einsum('nij,njk->nik', a, b, preferred_element_type=F32)
    new_states = []
    for d, (p1, p2, p3, gt, o_ref) in enumerate(((p1f, p2f, p3f, gtf, of_ref), (p1b, p2b, p3b, gtb, ob_ref))):
        s = s_ref[d].reshape(n, GDN_DK, GDN_DV)
        uw = p1[0].reshape(n, CH, 2 * CH)
        kq = p2[0].reshape(n, CH, 2 * CH)
        m1 = bmm(jnp.concatenate([uw[:, :, CH:], kq[:, :, CH:]], axis=1), s.astype(BF16))
        v_new = (uw[:, :, :CH].astype(F32) - m1[:, :CH]).astype(BF16)
        o = m1[:, CH:] + bmm(p3[0].reshape(n, CH, CH), v_new)
        o_ref[...] = o.reshape(nb, nh, CH, GDN_DV)
        g = gt[0].reshape(n, 8, LANE)[:, 0:1, :CH]
        new_states.append(s * g + bmm(kq[:, :, :CH], v_new))
    for d in range(2):
        s_ref[d] = new_states[d].reshape(nb, nh, GDN_DK, GDN_DV)


def _gdn_scan(p1, p2, p3, gt, S):
    _, B, H, T, _ = p1.shape
    n = T // CH
    n_lat = S // CH
    fwd = lambda i: (n_lat + i) % n
    bwd = lambda i: n - 1 - i
    specs = []
    for d, order in ((0, fwd), (1, bwd)):
        for w, rows in ((2 * CH, CH), (2 * CH, CH), (CH, CH), (LANE, 8)):
            specs.append(pl.BlockSpec((1, B, H, rows, w), functools.partial(lambda i, d, order: (d, 0, 0, order(i), 0), d=d, order=order)))
    out = jax.ShapeDtypeStruct((B, H, T, GDN_DV), F32)
    return pl.pallas_call(
        _gdn_scan_kernel,
        grid=(n,),
        in_specs=specs,
        out_specs=[pl.BlockSpec((B, H, CH, GDN_DV), lambda i: (0, 0, fwd(i), 0)),
                   pl.BlockSpec((B, H, CH, GDN_DV), lambda i: (0, 0, bwd(i), 0))],
        out_shape=[out, out],
        scratch_shapes=[pltpu.VMEM((2, B, H, GDN_DK, GDN_DV), F32)],
        compiler_params=_cparams(("arbitrary",)),
        name="gdn_scan",
    )(p1, p2, p3, gt, p1, p2, p3, gt)


def _rms(v, gain):
    return v * lax.rsqrt(jnp.mean(v * v, axis=-1, keepdims=True) + EPS) * gain


def _mix_out_kernel(x_ref, mod_ref, oa_ref, of_ref, ob_ref, z_ref, oc_ref, gw_ref, wo_ref,
                    post_ref, pre2_ref, rw_ref, rb_ref,
                    x1_ref, h2_ref, wd_ref, pos_ref, rankt_ref, cum_ref, carry_ref, *, tiles_per_super):
    g = pl.program_id(0) * pl.num_programs(1) + pl.program_id(1)

    @pl.when(g % tiles_per_super == 0)
    def _reset():
        carry_ref[...] = jnp.zeros_like(carry_ref)

    z = z_ref[0]
    gated = []
    for h in range(GDN_HEADS):
        o = of_ref[0, h] + ob_ref[0, h]
        zh = z[:, GDN_DV * h:GDN_DV * (h + 1)]
        gated.append(_rms(o, gw_ref[...]) * (zh * jax.nn.sigmoid(zh)))
    mixed = jnp.concatenate([oa_ref[0], jnp.concatenate(gated, axis=1).astype(BF16), oc_ref[0]], axis=1)
    y = jnp.dot(mixed, wo_ref[...], preferred_element_type=F32)
    x1 = x_ref[0] + mod_ref[0, 2:3, :] * _rms(y, post_ref[...])
    x1_ref[0] = x1
    h2 = (_rms(x1, pre2_ref[...]) * (1.0 + mod_ref[0, 4:5, :]) + mod_ref[0, 3:4, :]).astype(BF16)
    h2_ref[...] = h2

    scores = jax.nn.sigmoid(_nt_dot(rw_ref[...], h2))
    sel = scores + rb_ref[...]
    gsz = N_EXPERTS // MOE_GROUPS
    g3 = sel.reshape(MOE_GROUPS, gsz, TM)
    io = lax.broadcasted_iota(jnp.int32, g3.shape, 1)
    m1 = jnp.max(g3, axis=1, keepdims=True)
    i1 = jnp.min(jnp.where(g3 == m1, io, gsz), axis=1, keepdims=True)
    m2 = jnp.max(jnp.where(io == i1, -jnp.inf, g3), axis=1, keepdims=True)

    def top_mask(vals, k):
        n = vals.shape[0]
        idx = lax.broadcasted_iota(jnp.int32, vals.shape, 0)
        mask = jnp.zeros(vals.shape, F32)
        for _ in range(k):
            mx = jnp.max(vals, axis=0, keepdims=True)
            hit = idx == jnp.min(jnp.where(vals == mx, idx, n), axis=0, keepdims=True)
            mask = jnp.where(hit, 1.0, mask)
            vals = jnp.where(hit, -jnp.inf, vals)
        return mask

    gmask = top_mask((m1 + m2).reshape(MOE_GROUPS, TM), MOE_TOPK_GROUPS)
    masked = jnp.where(gmask.reshape(MOE_GROUPS, 1, TM) > 0.0, g3, -jnp.inf).reshape(N_EXPERTS, TM)
    smask = top_mask(masked, MOE_TOP_K)
    w = jnp.where(smask > 0.0, scores, 0.0)
    wn = w / jnp.sum(w, axis=0, keepdims=True) * ROUTED_SCALE
    wd_ref[...] = jnp.where(smask > 0.0, wn, -1.0)

    ci = lax.broadcasted_iota(jnp.int32, (TM, TM), 0)
    cj = lax.broadcasted_iota(jnp.int32, (TM, TM), 1)
    ut = jnp.where(ci <= cj, 1.0, 0.0).astype(BF16)
    cs = jnp.dot(smask.astype(BF16), ut, preferred_element_type=F32)
    carry = carry_ref[...]
    posf = jnp.where(smask > 0.0, carry + cs - 1.0, -1.0)
    pos_ref[...] = posf.astype(jnp.int32)
    rankt_ref[...] = posf.T
    carry = carry + jnp.broadcast_to(cs[:, TM - 1:TM], (N_EXPERTS, TM))
    carry_ref[...] = carry
    cum_ref[0] = carry[:, :LANE]


def _mix_out(xx, modl, oa, o_f, o_b, z, oc, gw, wo, post, pre2, rw, rb, n_lat, nt, tiles_per_super):
    B, T, D = xx.shape
    nb = modl.shape[0] - 1
    N = B * nt * TM

    def mod_map(b, t):
        return (jnp.where(t >= n_lat, nb, b), 0, 0)

    row = lambda w: pl.BlockSpec((1, TM, w), lambda b, t: (b, t, 0))
    flat = lambda b, t: (0, b * nt + t)
    kern = functools.partial(_mix_out_kernel, tiles_per_super=tiles_per_super)
    return pl.pallas_call(
        kern,
        grid=(B, nt),
        in_specs=[row(D), pl.BlockSpec((1, 6, D), mod_map), row(384),
                  pl.BlockSpec((1, GDN_HEADS, TM, GDN_DV), lambda b, t: (b, 0, t, 0)),
                  pl.BlockSpec((1, GDN_HEADS, TM, GDN_DV), lambda b, t: (b, 0, t, 0)),
                  row(384), row(256), _const_spec((1, GDN_DV)), _const_spec(wo.shape),
                  _const_spec((1, D)), _const_spec((1, D)), _const_spec(rw.shape), _const_spec((N_EXPERTS, 1))],
        out_specs=[row(D),
                   pl.BlockSpec((TM, D), lambda b, t: (b * nt + t, 0)),
                   pl.BlockSpec((N_EXPERTS, TM), flat),
                   pl.BlockSpec((N_EXPERTS, TM), flat),
                   pl.BlockSpec((TM, N_EXPERTS), lambda b, t: (b * nt + t, 0)),
                   pl.BlockSpec((1, N_EXPERTS, LANE), lambda b, t: (b * nt + t, 0, 0))],
        out_shape=[jax.ShapeDtypeStruct((B, nt * TM, D), F32),
                   jax.ShapeDtypeStruct((N, D), BF16),
                   jax.ShapeDtypeStruct((N_EXPERTS, N), F32),
                   jax.ShapeDtypeStruct((N_EXPERTS, N), jnp.int32),
                   jax.ShapeDtypeStruct((N, N_EXPERTS), F32),
                   jax.ShapeDtypeStruct((N // TM, N_EXPERTS, LANE), F32)],
        scratch_shapes=[pltpu.VMEM((N_EXPERTS, TM), F32)],
        compiler_params=_cparams(("arbitrary", "arbitrary")),
        name="mix_out_router",
    )(xx, modl, oa, o_f, o_b, z, oc, gw, wo, post, pre2, rw, rb)


MOE_BLOCK = 128


MOE_SUPER_TILES = 3
MOE_CHUNK_BLOCKS = 8
MOE_EXPERTS_PER_STEP = 4
MOE_NO_RANK = 1 << 20
MOE_VMEM_LIMIT = 62 * 1024 * 1024


def _moe_max_blocks(tps):
    b = tps * TM * MOE_TOP_K // MOE_BLOCK + N_EXPERTS
    return -(-b // MOE_CHUNK_BLOCKS) * MOE_CHUNK_BLOCKS


def _moe2_meta(cum, n_tiles, tps):
    ns = -(-n_tiles // tps)
    maxblk = _moe_max_blocks(tps)
    last = jnp.minimum((jnp.arange(ns) + 1) * tps - 1, n_tiles - 1)
    cnt = cum[last, :, 0].astype(jnp.int32)
    nslot = (cnt + MOE_BLOCK - 1) // MOE_BLOCK
    bend = jnp.cumsum(nslot, axis=1)
    bstart = bend - nslot
    b = jnp.arange(maxblk, dtype=jnp.int32)
    blk_e = jnp.minimum(jnp.sum(bend[:, None, :] <= b[None, :, None], axis=2), N_EXPERTS - 1)
    blk_k = b[None, :] - jnp.take_along_axis(bstart, blk_e, axis=1)
    blk_r0 = jnp.where(b[None, :] < bend[:, -1:], blk_k * MOE_BLOCK, MOE_NO_RANK)
    i32 = lambda a: a.reshape(-1).astype(jnp.int32)
    return i32(nslot), i32(bstart), i32(bend[:, -1]), i32(blk_e), i32(blk_r0)


def _moe2_kernel(nslot_ref, bstart_ref, nblk_ref, blke_ref, blkr_ref,
                 h2_ref, pos_ref, rankt_ref, wd_ref, wg_ref, wu_ref, wdn_ref, out_ref, yw_ref,
                 *, tps, maxblk):
    s = pl.program_id(0)
    j = pl.program_id(1)
    st = tps * TM

    @pl.when(jnp.logical_and(s == 0, j == 0))
    def _init():
        yw_ref[...] = jnp.zeros_like(yw_ref)

    ne = MOE_EXPERTS_PER_STEP
    n_expert_steps = N_EXPERTS // ne

    @pl.when(j < n_expert_steps)
    def _experts():
        p = s * N_EXPERTS + j * ne
        riota = lax.broadcasted_iota(jnp.int32, (MOE_BLOCK, st), 0)
        nslots = [nslot_ref[p + i] for i in range(ne)]

        def slot(k, c):
            hits = [pos_ref[i] == riota + k * MOE_BLOCK for i in range(ne)]
            onehot = jnp.concatenate([jnp.where(hit, 1.0, 0.0).astype(BF16) for hit in hits], axis=0)
            x = jnp.dot(onehot, h2_ref[...], preferred_element_type=F32).astype(BF16).reshape(ne, MOE_BLOCK, -1)
            bmm = lambda a, w_ref: jnp.einsum('eij,ejk->eik', a, w_ref[...], preferred_element_type=F32)
            hg = bmm(x, wg_ref)
            y = bmm((hg * jax.nn.sigmoid(hg) * bmm(x, wu_ref)).astype(BF16), wdn_ref)
            for i in range(ne):
                @pl.when(k < nslots[i])
                def _store():
                    wcol = jnp.sum(jnp.where(hits[i], wd_ref[i], 0.0), axis=1, keepdims=True)
                    row0 = pl.multiple_of((bstart_ref[p + i] + k) * MOE_BLOCK, MOE_BLOCK)
                    yw_ref[pl.ds(row0, MOE_BLOCK), :] = (y[i] * wcol).astype(BF16)
            return c

        n_slots = nslots[0]
        for i in range(1, ne):
            n_slots = jnp.maximum(n_slots, nslots[i])
        lax.fori_loop(0, n_slots, slot, 0)

    @pl.when(j == n_expert_steps)
    def _combine():
        pt = rankt_ref[...]
        lane_e = lax.broadcasted_iota(jnp.int32, (st, N_EXPERTS), 1)
        li = lax.broadcasted_iota(jnp.int32, (st, MOE_BLOCK), 1)
        rows = MOE_CHUNK_BLOCKS * MOE_BLOCK

        def chunk(c, acc):
            hts = []
            for bi in range(MOE_CHUNK_BLOCKS):
                b = s * maxblk + c * MOE_CHUNK_BLOCKS + bi
                col = jnp.sum(jnp.where(lane_e == blke_ref[b], pt, 0.0), axis=1, keepdims=True)
                hts.append(jnp.where(col == (li + blkr_ref[b]).astype(F32), 1.0, 0.0).astype(BF16))
            ht = jnp.concatenate(hts, axis=1)
            return acc + jnp.dot(ht, yw_ref[pl.ds(pl.multiple_of(c * rows, rows), rows), :],
                                 preferred_element_type=F32)

        n_chunks = (nblk_ref[s] + MOE_CHUNK_BLOCKS - 1) // MOE_CHUNK_BLOCKS
        out_ref[...] = lax.fori_loop(0, n_chunks, chunk, jnp.zeros(out_ref.shape, F32))


def _moe2_routed(h2, wd, pos, rankt, cum, wg, wu, wdn):
    N, D = h2.shape
    E = N_EXPERTS
    n_tiles = N // TM
    tps = min(MOE_SUPER_TILES, n_tiles)
    ns = -(-n_tiles // tps)
    st = tps * TM
    pad = ns * st - N
    if pad:
        h2 = jnp.pad(h2, ((0, pad), (0, 0)))
        wd = jnp.pad(wd, ((0, 0), (0, pad)))
        pos = jnp.pad(pos, ((0, 0), (0, pad)), constant_values=-1)
        rankt = jnp.pad(rankt, ((0, pad), (0, 0)), constant_values=-1.0)
    maxblk = _moe_max_blocks(tps)
    meta = _moe2_meta(cum, n_tiles, tps)
    ne = MOE_EXPERTS_PER_STEP
    nq = E // ne
    ex = lambda j: jnp.minimum(j, nq - 1)
    grid_spec = pltpu.PrefetchScalarGridSpec(
        num_scalar_prefetch=5,
        grid=(ns, nq + 1),
        in_specs=[pl.BlockSpec((st, D), lambda s, j, *_: (s, 0)),
                  pl.BlockSpec((ne, 1, st), lambda s, j, *_: (ex(j), 0, s)),
                  pl.BlockSpec((st, E), lambda s, j, *_: (s, 0)),
                  pl.BlockSpec((ne, 1, st), lambda s, j, *_: (ex(j), 0, s)),
                  pl.BlockSpec((ne, D, D_EXPERT), lambda s, j, *_: (ex(j), 0, 0)),
                  pl.BlockSpec((ne, D, D_EXPERT), lambda s, j, *_: (ex(j), 0, 0)),
                  pl.BlockSpec((ne, D_EXPERT, D), lambda s, j, *_: (ex(j), 0, 0))],
        out_specs=pl.BlockSpec((st, D), lambda s, j, *_: (s, 0)),
        scratch_shapes=[pltpu.VMEM((maxblk * MOE_BLOCK, D), BF16)],
    )
    return pl.pallas_call(
        functools.partial(_moe2_kernel, tps=tps, maxblk=maxblk),
        grid_spec=grid_spec,
        out_shape=jax.ShapeDtypeStruct((ns * st, D), F32),
        compiler_params=_cparams(("arbitrary", "arbitrary"), MOE_VMEM_LIMIT),
        name="moe_routed",
    )(*meta, h2, pos.reshape(E, 1, ns * st), rankt, wd.reshape(E, 1, ns * st), wg, wu, wdn)


def _ffn_out_kernel(x1_ref, mod_ref, h2_ref, routed_ref, sg_ref, su_ref, sd_ref, post_ref, o_ref):
    h2 = h2_ref[...]
    hg = jnp.dot(h2, sg_ref[...], preferred_element_type=F32)
    hu = jnp.dot(h2, su_ref[...], preferred_element_type=F32)
    f = jnp.dot((hg * jax.nn.sigmoid(hg) * hu).astype(BF16), sd_ref[...], preferred_element_type=F32) + routed_ref[...]
    o_ref[0] = x1_ref[0] + mod_ref[0, 5:6, :] * _rms(f, post_ref[...])


def _ffn_out(x1, modl, h2, routed, sg, su, sd, post, n_lat):
    B, Tn, D = x1.shape
    nt = Tn // TM
    nb = modl.shape[0] - 1

    def mod_map(b, t):
        return (jnp.where(t >= n_lat, nb, b), 0, 0)

    row = pl.BlockSpec((1, TM, D), lambda b, t: (b, t, 0))
    flat = pl.BlockSpec((TM, D), lambda b, t: (b * nt + t, 0))
    return pl.pallas_call(
        _ffn_out_kernel,
        grid=(B, nt),
        in_specs=[row, pl.BlockSpec((1, 6, D), mod_map), flat, flat,
                  _const_spec(sg.shape), _const_spec(su.shape), _const_spec(sd.shape), _const_spec((1, D))],
        out_specs=row,
        out_shape=jax.ShapeDtypeStruct((B, Tn, D), F32),
        compiler_params=_cparams(("parallel", "parallel")),
        name="ffn_out",
    )(x1, modl, h2, routed, sg, su, sd, post)


def kernel(x, c, ctx, c_ctx, ada_w, ada_b, mix_norm_pre, mix_norm_post, ffn_norm_pre, ffn_norm_post, w_in, w_out, swa_sink, gdn_conv_w, gdn_a_log, gdn_dt_bias, gdn_norm_w, mla_q_norm, mla_w_uq, mla_kv_norm, mla_w_ukv, router_w, router_bias, expert_w_gate, expert_w_up, expert_w_down, shared_w_gate, shared_w_up, shared_w_down):
    B, S, D = x.shape
    L = ctx.shape[1]
    assert L == TM and S % (2 * TM) == 0
    T = S + L
    n_lat = S // TM
    xx = jnp.concatenate([x, ctx], axis=1)
    tabs = _rope_tables(S, T)
    cvecs = jnp.concatenate([c, c_ctx[None], jnp.zeros((8 - B - 1, D), F32)], axis=0)
    mods = _modulation(cvecs, ada_w, ada_b).reshape(DEPTH, 8, 6, D)[:, :B + 1]
    for layer in range(DEPTH):
        xx = _layer(layer, xx, mods[layer], tabs, S, layer == DEPTH - 1,
                    mix_norm_pre, mix_norm_post, ffn_norm_pre, ffn_norm_post, w_in, w_out, swa_sink, gdn_conv_w,
                    gdn_a_log, gdn_dt_bias, gdn_norm_w, mla_q_norm, mla_w_uq, mla_kv_norm, mla_w_ukv, router_w,
                    router_bias, expert_w_gate, expert_w_up, expert_w_down, shared_w_gate, shared_w_up, shared_w_down)
    return xx


def _layer(layer, xx, modl, tabs, S, last, mix_norm_pre, mix_norm_post, ffn_norm_pre, ffn_norm_post, w_in, w_out,
           swa_sink, gdn_conv_w, gdn_a_log, gdn_dt_bias, gdn_norm_w, mla_q_norm, mla_w_uq, mla_kv_norm, mla_w_ukv,
           router_w, router_bias, expert_w_gate, expert_w_up, expert_w_down, shared_w_gate, shared_w_up, shared_w_down):
    B, T, D = xx.shape
    n_lat = S // TM
    nt = n_lat if last else T // TM
    qa, ka, va, gx, z, ba, qm, km, vm = _project(
        xx, modl, mix_norm_pre[layer][None], _prep_w_in(w_in[layer]), _prep_w_uq(mla_w_uq[layer]),
        _prep_w_ukv(mla_w_ukv[layer]), jnp.pad(mla_q_norm[layer], (0, 256 - MLA_Q_RANK))[None],
        mla_kv_norm[layer][None], tabs, n_lat)
    oa = _swa_attention(swa_sink[layer], qa, ka, va, S, nt)
    oc = _mla_attention(qm, km, vm, S, not last)
    gp = jnp.zeros((2, LANE), F32)
    g0 = _BA_LANE + 2 * GDN_HEADS
    gp = gp.at[0, g0:g0 + 2 * GDN_HEADS].set(gdn_a_log[layer].reshape(-1))
    gp = gp.at[1, g0:g0 + 2 * GDN_HEADS].set(gdn_dt_bias[layer].reshape(-1))
    p1, p2, p3, gt = _gdn_local(gx, gdn_conv_w[layer], ba, gp, n_lat)
    o_f, o_b = _gdn_scan(p1, p2, p3, gt, S)

    x1, h2, wd, pos, rankt, cum = _mix_out(
        xx, modl, oa, o_f, o_b, z, oc, gdn_norm_w[layer][None], _prep_w_out(w_out[layer]),
        mix_norm_post[layer][None], ffn_norm_pre[layer][None], router_w[layer].T.astype(BF16),
        router_bias[layer][:, None], n_lat, nt, min(MOE_SUPER_TILES, B * nt))
    routed = _moe2_routed(h2, wd, pos, rankt, cum, expert_w_gate[layer].astype(BF16),
                          expert_w_up[layer].astype(BF16), expert_w_down[layer].astype(BF16))
    return _ffn_out(x1, modl, h2, routed, shared_w_gate[layer].astype(BF16), shared_w_up[layer].astype(BF16),
                    shared_w_down[layer].astype(BF16), ffn_norm_post[layer][None], n_lat)
```

```python
import functools
import math

import numpy as np
import jax
import jax.numpy as jnp
from jax import lax
from jax.experimental import pallas as pl
from jax.experimental.pallas import tpu as pltpu

F32 = jnp.float32
BF16 = jnp.bfloat16

DEPTH = 2
GRID_W = 64
EPS = 1e-6
ROPE_BASE = 10000.0
HEAD_DIM = 64
SWA_HEADS = 6
SWA_KV_HEADS = 2
SWA_WINDOW = 128
GDN_HEADS = 6
GDN_DK = 64
GDN_DV = 64
GDN_CONV = 5
GDN_CHUNK = 64
MLA_HEADS = 4
MLA_Q_RANK = 192
MLA_KV_RANK = 128
MLA_NOPE = 64
MLA_ROPE = 32
MLA_V = 64
N_EXPERTS = 64
MOE_TOP_K = 8
MOE_GROUPS = 8
MOE_TOPK_GROUPS = 4
D_EXPERT = 256
ROUTED_SCALE = 2.5
LOG2_E = math.log2(math.e)

_SPLITS = (384, 128, 128, 1152, 384, 24, 192, 128, 32)
_OFF = np.concatenate([[0], np.cumsum(_SPLITS)]).tolist()
D_PROJ = _OFF[-1]

TM = 256
LANE = 128
VMEM_LIMIT = 56 * 1024 * 1024

_C_QA, _C_QAS, _C_KA, _C_KAS, _C_VA, _C_GDN, _C_Z, _C_CQ, _C_CKV, _C_KRB, _C_END = (
    0, 768, 1536, 1664, 1792, 1920, 3072, 3456, 3712, 3840, 3968)
_BA_LANE = 32


def _cparams(sem, vmem=VMEM_LIMIT):
    return pltpu.CompilerParams(dimension_semantics=sem, vmem_limit_bytes=vmem)


def _rope_partner(d, width):
    half, n = width // 2, width // 4
    i = d % half
    return (d // half) * half + (i + n if i < n else i - n)


def _take_cols(w, idx):
    idx = [int(i) for i in idx]
    pieces, start = [], 0
    for pos in range(1, len(idx) + 1):
        run_ends = pos == len(idx) or (idx[pos] != idx[pos - 1] + 1 if idx[pos - 1] >= 0 else idx[pos] >= 0) \
            or (idx[pos] < 0) != (idx[pos - 1] < 0)
        if run_ends:
            n = pos - start
            pieces.append(jnp.zeros((w.shape[0], n), w.dtype) if idx[start] < 0 else w[:, idx[start]:idx[start] + n])
            start = pos
    return jnp.concatenate(pieces, axis=1)


def _prep_w_in(w_in):
    cols = []
    for swap in (False, True):
        for h in range(SWA_HEADS):
            j = h // (SWA_HEADS // SWA_KV_HEADS)
            blk = [-1] * LANE
            for d in range(HEAD_DIM):
                blk[64 * j + d] = _OFF[0] + h * HEAD_DIM + (_rope_partner(d, HEAD_DIM) if swap else d)
            cols += blk
    for swap in (False, True):
        for j in range(SWA_KV_HEADS):
            cols += [_OFF[1] + j * HEAD_DIM + (_rope_partner(d, HEAD_DIM) if swap else d) for d in range(HEAD_DIM)]
    cols += list(range(_OFF[2], _OFF[3]))
    cols += list(range(_OFF[3], _OFF[4]))
    cols += list(range(_OFF[4], _OFF[5]))
    cols += list(range(_OFF[6], _OFF[7])) + [-1] * 64
    cols += list(range(_OFF[7], _OFF[8]))
    cols += list(range(_OFF[8], _OFF[9])) + list(range(_OFF[5], _OFF[6])) + [-1] * (LANE - 32 - 24)
    assert len(cols) == _C_END
    return _take_cols(w_in, cols).astype(BF16)


def _prep_w_uq(w_uq):
    cols = []
    for swap in (False, True):
        for h in range(MLA_HEADS):
            base = h * (MLA_NOPE + MLA_ROPE)
            blk = [-1] * LANE
            for d in range(MLA_NOPE):
                blk[d] = -1 if swap else base + d
            for r in range(MLA_ROPE):
                blk[MLA_NOPE + r] = base + MLA_NOPE + (_rope_partner(r, MLA_ROPE) if swap else r)
            cols += blk
    w = _take_cols(w_uq, cols)
    return jnp.pad(w, ((0, 256 - MLA_Q_RANK), (0, 0))).astype(BF16)


def _prep_w_ukv(w_ukv):
    kcols, vcols = [], []
    for h in range(MLA_HEADS):
        base = h * (MLA_NOPE + MLA_V)
        kcols += [base + d for d in range(MLA_NOPE)] + [-1] * 64
        vcols += [base + MLA_NOPE + d for d in range(MLA_V)]
    wk = _take_cols(w_ukv, kcols)
    wv = _take_cols(w_ukv, vcols)
    top = jnp.concatenate([wk, jnp.zeros_like(wk), wv], axis=1)
    place = np.zeros((128, 1280), np.float32)
    for h in range(MLA_HEADS):
        for r in range(MLA_ROPE):
            place[r, 128 * h + MLA_NOPE + r] = 1.0
            place[_rope_partner(r, MLA_ROPE), 512 + 128 * h + MLA_NOPE + r] = 1.0
    return jnp.concatenate([top, jnp.asarray(place)], axis=0).astype(BF16)


def _prep_w_out(w_out):
    rows = []
    G = SWA_HEADS // SWA_KV_HEADS
    for g in range(G):
        for j in range(SWA_KV_HEADS):
            rows += [(G * j + g) * HEAD_DIM + d for d in range(HEAD_DIM)]
    rows += list(range(SWA_HEADS * HEAD_DIM, w_out.shape[0]))
    return jnp.take(w_out, jnp.asarray(rows), axis=0).astype(BF16)


def _rope_tables(S, T):
    t = np.arange(S)
    row, col = t // GRID_W, t % GRID_W

    def tab(width, lanes):
        half, n = width // 2, width // 4
        c = np.ones((T, LANE), np.float64)
        s = np.zeros((T, LANE), np.float64)
        for lane, d in lanes:
            i = d % half
            pos = row if d < half else col
            ang =(pos.astype(np.float32) * np.float32(ROPE_BASE ** (-(i % n) / n))).astype(np.float64)
            c[:S, lane] = np.cos(ang)
            s[:S, lane] = -np.sin(ang) if i < n else np.sin(ang)
        return jnp.asarray(c, F32), jnp.asarray(s, F32)

    ca, sa = tab(HEAD_DIM, [(l, l % HEAD_DIM) for l in range(LANE)])
    cm, sm = tab(MLA_ROPE, [(MLA_NOPE + r, r) for r in range(MLA_ROPE)])
    return ca, sa, cm, sm


def _mod_kernel(c_ref, w_ref, b_ref, o_ref):
    cv = c_ref[...]
    a = (cv * jax.nn.sigmoid(cv)).astype(BF16)
    o_ref[0] = jnp.dot(a, w_ref[0].astype(BF16), preferred_element_type=F32) + b_ref[0]


def _modulation(cvecs, ada_w, ada_b):
    depth, D, N = ada_w.shape
    tn = 512
    return pl.pallas_call(
        _mod_kernel,
        grid=(depth, N // tn),
        in_specs=[pl.BlockSpec((8, D), lambda l, j: (0, 0)),
                  pl.BlockSpec((1, D, tn), lambda l, j: (l, 0, j)),
                  pl.BlockSpec((1, 1, tn), lambda l, j: (l, 0, j))],
        out_specs=pl.BlockSpec((1, 8, tn), lambda l, j: (l, 0, j)),
        out_shape=jax.ShapeDtypeStruct((depth, 8, N), F32),
        compiler_params=_cparams(("arbitrary", "arbitrary")),
        name="modulation",
    )(cvecs, ada_w, ada_b.reshape(depth, 1, N))


def _proj_kernel(x_ref, mod_ref, gain_ref, w_ref, wq2_ref, wk2_ref, qg_ref, kvg_ref,
                 ca_ref, sa_ref, cm_ref, sm_ref,
                 qa_ref, ka_ref, va_ref, gx_ref, z_ref, ba_ref, qm_ref, km_ref, vm_ref):
    x = x_ref[0]
    ms = jnp.mean(x * x, axis=-1, keepdims=True)
    h = x * lax.rsqrt(ms + EPS) * gain_ref[...]
    h = h * (1.0 + mod_ref[0, 1:2, :]) + mod_ref[0, 0:1, :]
    hb = h.astype(BF16)

    ca, sa, cm, sm = ca_ref[...], sa_ref[...], cm_ref[...], sm_ref[...]

    def rope(a, b, c, s):
        n = a.shape[1] // LANE
        return a * jnp.concatenate([c] * n, axis=1) + b * jnp.concatenate([s] * n, axis=1)

    p1 = jnp.dot(hb, w_ref[:, _C_QA:_C_GDN], preferred_element_type=F32)
    qa = rope(p1[:, _C_QA:_C_QAS], p1[:, _C_QAS:_C_KA], ca, sa)
    qa_ref[0] = (qa * (HEAD_DIM ** -0.5)).astype(BF16)
    ka_ref[0] = rope(p1[:, _C_KA:_C_KAS], p1[:, _C_KAS:_C_VA], ca, sa).astype(BF16)
    va_ref[0] = p1[:, _C_VA:_C_GDN].astype(BF16)

    p2 = jnp.dot(hb, w_ref[:, _C_GDN:_C_Z], preferred_element_type=F32)
    for j in range(3 * GDN_HEADS):
        gx_ref[0, j] = p2[:, 64 * j:64 * j + 64]

    p3 = jnp.dot(hb, w_ref[:, _C_Z:_C_END], preferred_element_type=F32)
    z_ref[0] = p3[:, 0:_C_CQ - _C_Z]
    cq = p3[:, _C_CQ - _C_Z:_C_CKV - _C_Z]
    ckv = p3[:, _C_CKV - _C_Z:_C_KRB - _C_Z]
    krb = p3[:, _C_KRB - _C_Z:]
    ba_ref[0] = krb

    cqn = cq * lax.rsqrt(jnp.sum(cq * cq, axis=-1, keepdims=True) * (1.0 / MLA_Q_RANK) + EPS) * qg_ref[...]
    e = jnp.dot(cqn.astype(BF16), wq2_ref[...], preferred_element_type=F32)
    qm = rope(e[:, :512], e[:, 512:], cm, sm)
    qm_ref[0] = (qm * (LOG2_E * (MLA_NOPE + MLA_ROPE) ** -0.5)).astype(BF16)

    ckvn = ckv * lax.rsqrt(jnp.mean(ckv * ckv, axis=-1, keepdims=True) + EPS) * kvg_ref[...]
    lhs2 = jnp.concatenate([ckvn.astype(BF16), krb.astype(BF16)], axis=1)
    e2 = jnp.dot(lhs2, wk2_ref[...], preferred_element_type=F32)
    km_ref[0] = rope(e2[:, :512], e2[:, 512:1024], cm, sm).astype(BF16)
    vm_ref[0] = e2[:, 1024:].astype(BF16)


def _const_spec(shape):
    nd = len(shape)
    return pl.BlockSpec(shape, lambda *_: (0,) * nd)


def _project(xx, modl, gain, w_main, wq2, wk2, qg, kvg, tabs, n_lat_tiles):
    B, T, D = xx.shape
    nt = T // TM
    nb = modl.shape[0] - 1

    def mod_map(b, t):
        return (jnp.where(t >= n_lat_tiles, nb, b), 0, 0)

    row = lambda w: pl.BlockSpec((1, TM, w), lambda b, t: (b, t, 0))
    tab = pl.BlockSpec((TM, LANE), lambda b, t: (t, 0))
    out_shapes = [
        jax.ShapeDtypeStruct((B, T, 768), BF16),
        jax.ShapeDtypeStruct((B, T, 128), BF16),
        jax.ShapeDtypeStruct((B, T, 128), BF16),
        jax.ShapeDtypeStruct((B, 18, T, 64), F32),
        jax.ShapeDtypeStruct((B, T, 384), F32),
        jax.ShapeDtypeStruct((B, T, 128), F32),
        jax.ShapeDtypeStruct((B, T, 512), BF16),
        jax.ShapeDtypeStruct((B, T, 512), BF16),
        jax.ShapeDtypeStruct((B, T, 256), BF16),
    ]
    out_specs = [row(768), row(128), row(128),
                 pl.BlockSpec((1, 18, TM, 64), lambda b, t: (b, 0, t, 0)),
                 row(384), row(128), row(512), row(512), row(256)]
    return pl.pallas_call(
        _proj_kernel,
        grid=(B, nt),
        in_specs=[row(D), pl.BlockSpec((1, 6, D), mod_map), _const_spec((1, D)),
                  _const_spec(w_main.shape), _const_spec(wq2.shape), _const_spec(wk2.shape),
                  _const_spec((1, 256)), _const_spec((1, 128)), tab, tab, tab, tab],
        out_specs=out_specs,
        out_shape=out_shapes,
        compiler_params=_cparams(("parallel", "parallel")),
        name="in_proj",
    )(xx, modl, gain, w_main, wq2, wk2, qg, kvg, *tabs)


def _nt_dot(a, b):
    return lax.dot_general(a, b, (((1,), (1,)), ((), ())), preferred_element_type=F32)


def _swa_kernel(sink_ref, q_ref, k_ref, v_ref, o_ref, *, S, n_lat):
    i = pl.program_id(1)
    G = SWA_HEADS // SWA_KV_HEADS
    W = 2 * TM
    lane = lax.broadcasted_iota(jnp.int32, (TM, LANE), 1)
    kc = k_ref[0, pl.ds(S, TM), :]
    vc = v_ref[0, pl.ds(S, TM), :]

    def heads(local):
        outs = []
        for g in range(G):
            og = []
            for j in range(SWA_KV_HEADS):
                h = G * j + g
                q = q_ref[0, :, LANE * h:LANE * (h + 1)]
                sink = sink_ref[h]
                s_ctx = _nt_dot(q, kc)
                m = jnp.maximum(jnp.max(s_ctx, axis=-1, keepdims=True), sink)
                if local is not None:
                    kw, vw, valid = local
                    s_loc = jnp.where(valid, _nt_dot(q, kw), -jnp.inf)
                    m = jnp.maximum(m, jnp.max(s_loc, axis=-1, keepdims=True))
                p_ctx = jnp.exp(s_ctx - m)
                den = jnp.sum(p_ctx, axis=-1, keepdims=True) + jnp.exp(sink - m)
                o = jnp.dot(p_ctx.astype(BF16), vc, preferred_element_type=F32)
                if local is not None:
                    p_loc = jnp.exp(s_loc - m)
                    den = den + jnp.sum(p_loc, axis=-1, keepdims=True)
                    o = o + jnp.dot(p_loc.astype(BF16), vw, preferred_element_type=F32)
                og.append(o * (1.0 / den))
            outs.append(jnp.where(lane < HEAD_DIM, og[0], og[1]))
        o_ref[0] = jnp.concatenate(outs, axis=1).astype(BF16)

    @pl.when(i < n_lat)
    def _latent():
        start = pl.multiple_of(jnp.clip(i * TM - SWA_WINDOW, 0, S - W), LANE)
        kw = k_ref[0, pl.ds(start, W), :]
        vw = v_ref[0, pl.ds(start, W), :]
        qpos = i * TM + lax.broadcasted_iota(jnp.int32, (TM, W), 0)
        kpos = start + lax.broadcasted_iota(jnp.int32, (TM, W), 1)
        heads((kw, vw, jnp.abs(qpos - kpos) <= SWA_WINDOW))

    @pl.when(i >= n_lat)
    def _context():
        heads(None)


def _swa_attention(sink, qa, ka, va, S, nt):
    B, T, _ = qa.shape
    kern = functools.partial(_swa_kernel, S=S, n_lat=S // TM)
    return pl.pallas_call(
        kern,
        grid=(B, nt),
        in_specs=[pl.BlockSpec(memory_space=pltpu.SMEM),
                  pl.BlockSpec((1, TM, 768), lambda b, t: (b, t, 0)),
                  pl.BlockSpec((1, T, 128), lambda b, t: (b, 0, 0)),
                  pl.BlockSpec((1, T, 128), lambda b, t: (b, 0, 0))],
        out_specs=pl.BlockSpec((1, TM, 384), lambda b, t: (b, t, 0)),
        out_shape=jax.ShapeDtypeStruct((B, nt * TM, 384), BF16),
        compiler_params=_cparams(("parallel", "parallel")),
        name="swa_attention",
    )(sink, qa, ka, va)


MLA_SUBSTEPS = 2
MLA_KEY_CHUNK = 4608


def _mla_kernel(q_ref, k_ref, v_ref, o_ref, *, S, tq, latent):
    lane = lax.broadcasted_iota(jnp.int32, (tq, LANE), 1)
    n_keys = S + TM
    n_chunks = min(n for n in range(1, 65) if n_keys % (n * LANE) == 0 and n_keys // n <= MLA_KEY_CHUNK) if latent else 0
    chunk = n_keys // n_chunks if latent else 0
    blocks_per_sub = chunk // LANE // MLA_SUBSTEPS
    sub_sizes = [blocks_per_sub * LANE] * (MLA_SUBSTEPS - 1) + [chunk - blocks_per_sub * LANE * (MLA_SUBSTEPS - 1)]

    outs = []
    for h in range(MLA_HEADS):
        q = q_ref[0, :, LANE * h:LANE * (h + 1)]

        def step(off, size, carry):
            m, l, acc = carry
            s = _nt_dot(q, k_ref[0, pl.ds(off, size), LANE * h:LANE * (h + 1)])
            blocks = [s[:, LANE * j:LANE * (j + 1)] for j in range(size // LANE)]
            mx = blocks[0]
            for blk in blocks[1:]:
                mx = jnp.maximum(mx, blk)
            m_new = jnp.maximum(m, jnp.broadcast_to(jnp.max(mx, axis=-1, keepdims=True), (tq, LANE)))
            alpha = jnp.exp2(m - m_new)
            ps = [jnp.exp2(blk - m_new) for blk in blocks]
            l = alpha * l
            for p in ps:
                l = l + p
            p = jnp.concatenate(ps, axis=1).astype(BF16)
            v = v_ref[0, pl.ds(off, size), LANE * (h // 2):LANE * (h // 2 + 1)]
            return m_new, l, alpha * acc + jnp.dot(p, v, preferred_element_type=F32)

        def body(c, cr):
            off = pl.multiple_of(c * chunk, chunk)
            for u in range(MLA_SUBSTEPS):
                cr = step(pl.multiple_of(off + sum(sub_sizes[:u]), LANE), sub_sizes[u], cr)
            return cr

        zero = jnp.zeros((tq, LANE), F32)
        carry = (jnp.full((tq, LANE), -jnp.inf, F32), zero, zero)
        m, l, acc = lax.fori_loop(0, n_chunks, body, carry) if latent else step(S, TM, carry)
        outs.append(acc * (1.0 / jnp.sum(l, axis=-1, keepdims=True)))
    o_ref[0] = jnp.concatenate([jnp.where(lane < MLA_V, outs[0], outs[1]),
                                jnp.where(lane < MLA_V, outs[2], outs[3])], axis=1).astype(BF16)


MLA_Q_TILE = 256


def _mla_attention(qm, km, vm, S, with_context):
    B, T, _ = qm.shape

    def call(tq, n_q, q_block0, latent):
        return pl.pallas_call(
            functools.partial(_mla_kernel, S=S, tq=tq, latent=latent),
            grid=(B, n_q),
            in_specs=[pl.BlockSpec((1, tq, 512), lambda b, t: (b, q_block0 + t, 0)),
                      pl.BlockSpec((1, T, 512), lambda b, t: (b, 0, 0)),
                      pl.BlockSpec((1, T, 256), lambda b, t: (b, 0, 0))],
            out_specs=pl.BlockSpec((1, tq, 256), lambda b, t: (b, t, 0)),
            out_shape=jax.ShapeDtypeStruct((B, n_q * tq, 256), BF16),
            compiler_params=_cparams(("parallel", "parallel")),
            name="mla_attention" if latent else "mla_context",
        )(qm, km, vm)

    out = call(MLA_Q_TILE, S // MLA_Q_TILE, 0, True)
    if with_context:
        out = jnp.concatenate([out, call(TM, 1, S // TM, False)], axis=1)
    return out


CH = GDN_CHUNK
NCH = TM // CH


def _split3(a):
    hi = a.astype(BF16)
    r = a - hi.astype(F32)
    mid = r.astype(BF16)
    return hi, mid, (r - mid.astype(F32)).astype(BF16)


def _bmm3(a, b):
    ah, al, _ = _split3(a)
    bh, bl, _ = _split3(b)
    lhs = jnp.concatenate([ah, al, ah], axis=-1)
    rhs = jnp.concatenate([bh, bh, bl], axis=1)
    return jnp.einsum('cij,cjk->cik', lhs, rhs, preferred_element_type=F32)


GDN_HEADS_PER_STEP = 3


def _gdn_local_kernel(x_ref, xp_ref, xn_ref, cw_ref, ba_ref, gp_ref,
                      p1_ref, p2_ref, p3_ref, gt_ref, xs_ref, *, n_lat):
    t = pl.program_id(2)
    has_prev = jnp.logical_and(t > 0, t != n_lat)
    has_next = t < n_lat - 1

    baf = ba_ref[0]
    lane = lax.broadcasted_iota(jnp.int32, (TM, LANE), 1)
    beta_all = jax.nn.sigmoid(baf)
    xg = baf + gp_ref[1:2, :]
    g_all = -jnp.exp(gp_ref[0:1, :]) * (jnp.maximum(xg, 0.0) + jnp.log(1.0 + jnp.exp(-jnp.abs(xg))))

    def col(a, idx):
        cvec = jnp.sum(jnp.where(lane == idx, a, 0.0), axis=1, keepdims=True)
        return jnp.broadcast_to(cvec, (TM, CH)).reshape(NCH, CH, CH)

    ii = lax.broadcasted_iota(jnp.int32, (CH, CH), 0)
    jj = lax.broadcasted_iota(jnp.int32, (CH, CH), 1)
    eye = (ii == jj).astype(F32)[None]
    nt = lambda a, b: jnp.einsum('cid,cjd->cij', a, b, preferred_element_type=F32)

    groups = []
    for hh in range(GDN_HEADS_PER_STEP):
        h = pl.program_id(1) * GDN_HEADS_PER_STEP + hh
        parts = []
        for part in range(3):
            xs_ref[hh, part, 0:8, :] = jnp.where(has_prev, xp_ref[0, part, hh], 0.0)
            xs_ref[hh, part, 8:8 + TM, :] = x_ref[0, part, hh]
            xs_ref[hh, part, 8 + TM:16 + TM, :] = jnp.where(has_next, xn_ref[0, part, hh], 0.0)
            acc = jnp.zeros((TM, GDN_DK), F32)
            for k in range(GDN_CONV):
                acc = acc + cw_ref[hh, part, k:k + 1, :] * xs_ref[hh, part, pl.ds(8 - GDN_CONV // 2 + k, TM), :]
            parts.append(acc * jax.nn.sigmoid(acc))
        q_, k_, v = parts
        q = q_ * lax.rsqrt(jnp.sum(q_ * q_, axis=-1, keepdims=True) + EPS) * (GDN_DK ** -0.5)
        k = k_ * lax.rsqrt(jnp.sum(k_ * k_, axis=-1, keepdims=True) + EPS)
        q3, k3, v3 = (a.reshape(NCH, CH, GDN_DK) for a in (q, k, v))
        kb, qb = k3.astype(BF16), q3.astype(BF16)
        kk, qk = nt(kb, kb), nt(qb, kb)
        for d in range(2):
            groups.append(dict(
                q3=q3, k3=k3, v3=v3, kk=kk, qk=qk,
                beta=col(beta_all, _BA_LANE + GDN_HEADS * d + h),
                g=col(g_all, _BA_LANE + 2 * GDN_HEADS + GDN_HEADS * d + h)))

    ngrp = len(groups)
    stack = lambda key: jnp.concatenate([grp[key] for grp in groups], axis=0)
    per_dir = lambda lo_, up_: jnp.concatenate(
        [jnp.broadcast_to((lo_ if gi % 2 == 0 else up_)[None], (NCH, CH, CH)) for gi in range(ngrp)], axis=0)
    incl = per_dir(jj <= ii, jj >= ii)
    strict = per_dir(jj < ii, jj > ii)
    q3, k3, v3, kk, qk, beta, g = (stack(key) for key in ('q3', 'k3', 'v3', 'kk', 'qk', 'beta', 'g'))
    nb = ngrp * NCH
    tri = incl.astype(BF16)
    gcx = jnp.einsum('cij,cjl->cil', jnp.concatenate([tri] * 3, axis=-1),
                     jnp.concatenate(_split3(g), axis=1), preferred_element_type=F32)
    gcr = jnp.stack([gcx[c].T for c in range(nb)], axis=0)
    decay = jnp.exp(jnp.where(incl, gcx - gcr, -jnp.inf))
    lm = jnp.where(strict, beta * kk * decay, 0.0)
    egc = jnp.exp(gcx)
    rhs = jnp.concatenate([v3 * beta, k3 * beta * egc], axis=-1)
    p = -lm
    tinv = eye + p
    for _ in range(5):
        p = _bmm3(p, p)
        tinv = tinv + _bmm3(tinv, p)
    uw = _bmm3(tinv, rhs)
    gce = jnp.concatenate(
        [gcx[NCH * gi:NCH * (gi + 1), (CH - 1 if gi % 2 == 0 else 0):(CH if gi % 2 == 0 else 1), :] for gi in range(ngrp)],
        axis=0)
    k_out = k3 * jnp.exp(gce - gcx)
    k_out_t = jnp.stack([k_out[c].T for c in range(nb)], axis=0)
    kq = jnp.concatenate([k_out_t, q3 * egc], axis=-1)
    intra = qk * decay
    gte = jnp.exp(gce)
    gtb = jnp.broadcast_to(jnp.concatenate([gte, gte], axis=-1), (nb, 8, LANE))
    for gi in range(ngrp):
        hh, d = gi // 2, gi % 2
        sl = slice(NCH * gi, NCH * (gi + 1))
        p1_ref[d, 0, hh] = uw[sl].reshape(TM, 2 * CH).astype(BF16)
        p2_ref[d, 0, hh] = kq[sl].reshape(TM, 2 * CH).astype(BF16)
        p3_ref[d, 0, hh] = intra[sl].reshape(TM, CH).astype(BF16)
        gt_ref[d, 0, hh] = gtb[sl].reshape(NCH * 8, LANE)


def _gdn_local(gx, conv_w, ba, gp, n_lat):
    B, _, T, _ = gx.shape
    H = GDN_HEADS
    hp = GDN_HEADS_PER_STEP
    nt = T // TM
    gx5 = gx.reshape(B, 3, H, T, GDN_DK)
    cw = jnp.transpose(conv_w.reshape(GDN_CONV, 3, H, GDN_DK), (2, 1, 0, 3))
    kern = functools.partial(_gdn_local_kernel, n_lat=n_lat)
    r8 = TM // 8
    big = lambda w, dt: jax.ShapeDtypeStruct((2, B, H, T, w), dt)
    ospec = lambda w: pl.BlockSpec((2, 1, hp, TM, w), lambda b, h, t: (0, b, h, t, 0))
    return pl.pallas_call(
        kern,
        grid=(B, H // hp, nt),
        in_specs=[pl.BlockSpec((1, 3, hp, TM, GDN_DK), lambda b, h, t: (b, 0, h, t, 0)),
                  pl.BlockSpec((1, 3, hp, 8, GDN_DK), lambda b, h, t: (b, 0, h, jnp.maximum(t * r8 - 1, 0), 0)),
                  pl.BlockSpec((1, 3, hp, 8, GDN_DK), lambda b, h, t: (b, 0, h, jnp.minimum((t + 1) * r8, T // 8 - 1), 0)),
                  pl.BlockSpec((hp, 3, GDN_CONV, GDN_DK), lambda b, h, t: (h, 0, 0, 0)),
                  pl.BlockSpec((1, TM, LANE), lambda b, h, t: (b, t, 0)),
                  pl.BlockSpec((2, LANE), lambda b, h, t: (0, 0))],
        out_specs=[ospec(2 * CH), ospec(2 * CH), ospec(CH),
                   pl.BlockSpec((2, 1, hp, NCH * 8, LANE), lambda b, h, t: (0, b, h, t, 0))],
        out_shape=[big(2 * CH, BF16), big(2 * CH, BF16), big(CH, BF16),
                   jax.ShapeDtypeStruct((2, B, H, (T // CH) * 8, LANE), F32)],
        scratch_shapes=[pltpu.VMEM((hp, 3, TM + 16, GDN_DK), F32)],
        compiler_params=_cparams(("parallel", "parallel", "parallel")),
        name="gdn_local",
    )(gx5, gx5, gx5, cw, ba, gp)


def _gdn_scan_kernel(p1f, p2f, p3f, gtf, p1b, p2b, p3b, gtb, of_ref, ob_ref, s_ref):
    @pl.when(pl.program_id(0) == 0)
    def _init():
        s_ref[...] = jnp.zeros_like(s_ref)

    nb, nh = s_ref.shape[1], s_ref.shape[2]
    n = nb * nh
    bmm = lambda a, b: jnp.einsum('nij,njk->nik', a, b, preferred_element_type=F32)
    new_states = []
    for d, (p1, p2, p3, gt, o_ref) in enumerate(((p1f, p2f, p3f, gtf, of_ref), (p1b, p2b, p3b, gtb, ob_ref))):
        s = s_ref[d].reshape(n, GDN_DK, GDN_DV)
        uw = p1[0].reshape(n, CH, 2 * CH)
        kq = p2[0].reshape(n, CH, 2 * CH)
        m1 = bmm(jnp.concatenate([uw[:, :, CH:], kq[:, :, CH:]], axis=1), s.astype(BF16))
        v_new = (uw[:, :, :CH].astype(F32) - m1[:, :CH]).astype(BF16)
        o = m1[:, CH:] + bmm(p3[0].reshape(n, CH, CH), v_new)
        o_ref[...] = o.reshape(nb, nh, CH, GDN_DV)
        g = gt[0].reshape(n, 8, LANE)[:, 0:1, :CH]
        new_states.append(s * g + bmm(kq[:, :, :CH], v_new))
    for d in range(2):
        s_ref[d] = new_states[d].reshape(nb, nh, GDN_DK, GDN_DV)


def _gdn_scan(p1, p2, p3, gt, S):
    _, B, H, T, _ = p1.shape
    n = T // CH
    n_lat = S // CH
    fwd = lambda i: (n_lat + i) % n
    bwd = lambda i: n - 1 - i
    specs = []
    for d, order in ((0, fwd), (1, bwd)):
        for w, rows in ((2 * CH, CH), (2 * CH, CH), (CH, CH), (LANE, 8)):
            specs.append(pl.BlockSpec((1, B, H, rows, w), functools.partial(lambda i, d, order: (d, 0, 0, order(i), 0), d=d, order=order)))
    out = jax.ShapeDtypeStruct((B, H, T, GDN_DV), F32)
    return pl.pallas_call(
        _gdn_scan_kernel,
        grid=(n,),
        in_specs=specs,
        out_specs=[pl.BlockSpec((B, H, CH, GDN_DV), lambda i: (0, 0, fwd(i), 0)),
                   pl.BlockSpec((B, H, CH, GDN_DV), lambda i: (0, 0, bwd(i), 0))],
        out_shape=[out, out],
        scratch_shapes=[pltpu.VMEM((2, B, H, GDN_DK, GDN_DV), F32)],
        compiler_params=_cparams(("arbitrary",)),
        name="gdn_scan",
    )(p1, p2, p3, gt, p1, p2, p3, gt)


def _rms(v, gain):
    return v * lax.rsqrt(jnp.mean(v * v, axis=-1, keepdims=True) + EPS) * gain


def _mix_out_kernel(x_ref, mod_ref, oa_ref, of_ref, ob_ref, z_ref, oc_ref, gw_ref, wo_ref,
                    post_ref, pre2_ref, rw_ref, rb_ref,
                    x1_ref, h2_ref, wd_ref, pos_ref, rankt_ref, cum_ref, carry_ref, *, tiles_per_super):
    g = pl.program_id(0) * pl.num_programs(1) + pl.program_id(1)

    @pl.when(g % tiles_per_super == 0)
    def _reset():
        carry_ref[...] = jnp.zeros_like(carry_ref)

    z = z_ref[0]
    gated = []
    for h in range(GDN_HEADS):
        o = of_ref[0, h] + ob_ref[0, h]
        zh = z[:, GDN_DV * h:GDN_DV * (h + 1)]
        gated.append(_rms(o, gw_ref[...]) * (zh * jax.nn.sigmoid(zh)))
    mixed = jnp.concatenate([oa_ref[0], jnp.concatenate(gated, axis=1).astype(BF16), oc_ref[0]], axis=1)
    y = jnp.dot(mixed, wo_ref[...], preferred_element_type=F32)
    x1 = x_ref[0] + mod_ref[0, 2:3, :] * _rms(y, post_ref[...])
    x1_ref[0] = x1
    h2 = (_rms(x1, pre2_ref[...]) * (1.0 + mod_ref[0, 4:5, :]) + mod_ref[0, 3:4, :]).astype(BF16)
    h2_ref[...] = h2

    scores = jax.nn.sigmoid(_nt_dot(rw_ref[...], h2))
    sel = scores + rb_ref[...]
    gsz = N_EXPERTS // MOE_GROUPS
    g3 = sel.reshape(MOE_GROUPS, gsz, TM)
    io = lax.broadcasted_iota(jnp.int32, g3.shape, 1)
    m1 = jnp.max(g3, axis=1, keepdims=True)
    i1 = jnp.min(jnp.where(g3 == m1, io, gsz), axis=1, keepdims=True)
    m2 = jnp.max(jnp.where(io == i1, -jnp.inf, g3), axis=1, keepdims=True)

    def top_mask(vals, k):
        n = vals.shape[0]
        idx = lax.broadcasted_iota(jnp.int32, vals.shape, 0)
        mask = jnp.zeros(vals.shape, F32)
        for _ in range(k):
            mx = jnp.max(vals, axis=0, keepdims=True)
            hit = idx == jnp.min(jnp.where(vals == mx, idx, n), axis=0, keepdims=True)
            mask = jnp.where(hit, 1.0, mask)
            vals = jnp.where(hit, -jnp.inf, vals)
        return mask

    gmask = top_mask((m1 + m2).reshape(MOE_GROUPS, TM), MOE_TOPK_GROUPS)
    masked = jnp.where(gmask.reshape(MOE_GROUPS, 1, TM) > 0.0, g3, -jnp.inf).reshape(N_EXPERTS, TM)
    smask = top_mask(masked, MOE_TOP_K)
    w = jnp.where(smask > 0.0, scores, 0.0)
    wn = w / jnp.sum(w, axis=0, keepdims=True) * ROUTED_SCALE
    wd_ref[...] = jnp.where(smask > 0.0, wn, -1.0)

    ci = lax.broadcasted_iota(jnp.int32, (TM, TM), 0)
    cj = lax.broadcasted_iota(jnp.int32, (TM, TM), 1)
    ut = jnp.where(ci <= cj, 1.0, 0.0).astype(BF16)
    cs = jnp.dot(smask.astype(BF16), ut, preferred_element_type=F32)
    carry = carry_ref[...]
    posf = jnp.where(smask > 0.0, carry + cs - 1.0, -1.0)
    pos_ref[...] = posf.astype(jnp.int32)
    rankt_ref[...] = posf.T
    carry = carry + jnp.broadcast_to(cs[:, TM - 1:TM], (N_EXPERTS, TM))
    carry_ref[...] = carry
    cum_ref[0] = carry[:, :LANE]


def _mix_out(xx, modl, oa, o_f, o_b, z, oc, gw, wo, post, pre2, rw, rb, n_lat, nt, tiles_per_super):
    B, T, D = xx.shape
    nb = modl.shape[0] - 1
    N = B * nt * TM

    def mod_map(b, t):
        return (jnp.where(t >= n_lat, nb, b), 0, 0)

    row = lambda w: pl.BlockSpec((1, TM, w), lambda b, t: (b, t, 0))
    flat = lambda b, t: (0, b * nt + t)
    kern = functools.partial(_mix_out_kernel, tiles_per_super=tiles_per_super)
    return pl.pallas_call(
        kern,
        grid=(B, nt),
        in_specs=[row(D), pl.BlockSpec((1, 6, D), mod_map), row(384),
                  pl.BlockSpec((1, GDN_HEADS, TM, GDN_DV), lambda b, t: (b, 0, t, 0)),
                  pl.BlockSpec((1, GDN_HEADS, TM, GDN_DV), lambda b, t: (b, 0, t, 0)),
                  row(384), row(256), _const_spec((1, GDN_DV)), _const_spec(wo.shape),
                  _const_spec((1, D)), _const_spec((1, D)), _const_spec(rw.shape), _const_spec((N_EXPERTS, 1))],
        out_specs=[row(D),
                   pl.BlockSpec((TM, D), lambda b, t: (b * nt + t, 0)),
                   pl.BlockSpec((N_EXPERTS, TM), flat),
                   pl.BlockSpec((N_EXPERTS, TM), flat),
                   pl.BlockSpec((TM, N_EXPERTS), lambda b, t: (b * nt + t, 0)),
                   pl.BlockSpec((1, N_EXPERTS, LANE), lambda b, t: (b * nt + t, 0, 0))],
        out_shape=[jax.ShapeDtypeStruct((B, nt * TM, D), F32),
                   jax.ShapeDtypeStruct((N, D), BF16),
                   jax.ShapeDtypeStruct((N_EXPERTS, N), F32),
                   jax.ShapeDtypeStruct((N_EXPERTS, N), jnp.int32),
                   jax.ShapeDtypeStruct((N, N_EXPERTS), F32),
                   jax.ShapeDtypeStruct((N // TM, N_EXPERTS, LANE), F32)],
        scratch_shapes=[pltpu.VMEM((N_EXPERTS, TM), F32)],
        compiler_params=_cparams(("arbitrary", "arbitrary")),
        name="mix_out_router",
    )(xx, modl, oa, o_f, o_b, z, oc, gw, wo, post, pre2, rw, rb)


MOE_BLOCK = 128


MOE_SUPER_TILES = 3
MOE_CHUNK_BLOCKS = 16
MOE_EXPERTS_PER_STEP = 4
MOE_NO_RANK = 1 << 20
MOE_VMEM_LIMIT = 62 * 1024 * 1024


def _moe_max_blocks(tps):
    b = tps * TM * MOE_TOP_K // MOE_BLOCK + N_EXPERTS
    return -(-b // MOE_CHUNK_BLOCKS) * MOE_CHUNK_BLOCKS


def _moe2_meta(cum, n_tiles, tps):
    ns = -(-n_tiles // tps)
    maxblk = _moe_max_blocks(tps)
    last = jnp.minimum((jnp.arange(ns) + 1) * tps - 1, n_tiles - 1)
    cnt = cum[last, :, 0].astype(jnp.int32)
    nslot = (cnt + MOE_BLOCK - 1) // MOE_BLOCK
    bend = jnp.cumsum(nslot, axis=1)
    bstart = bend - nslot
    b = jnp.arange(maxblk, dtype=jnp.int32)
    blk_e = jnp.minimum(jnp.sum(bend[:, None, :] <= b[None, :, None], axis=2), N_EXPERTS - 1)
    blk_k = b[None, :] - jnp.take_along_axis(bstart, blk_e, axis=1)
    blk_r0 = jnp.where(b[None, :] < bend[:, -1:], blk_k * MOE_BLOCK, MOE_NO_RANK)
    i32 = lambda a: a.reshape(-1).astype(jnp.int32)
    return i32(nslot), i32(bstart), i32(bend[:, -1]), i32(blk_e), i32(blk_r0)


def _moe2_kernel(nslot_ref, bstart_ref, nblk_ref, blke_ref, blkr_ref,
                 h2_ref, pos_ref, rankt_ref, wd_ref, wg_ref, wu_ref, wdn_ref, out_ref, yw_ref,
                 *, tps, maxblk):
    s = pl.program_id(0)
    j = pl.program_id(1)
    st = tps * TM

    @pl.when(jnp.logical_and(s == 0, j == 0))
    def _init():
        yw_ref[...] = jnp.zeros_like(yw_ref)

    ne = MOE_EXPERTS_PER_STEP
    n_expert_steps = N_EXPERTS // ne

    @pl.when(j < n_expert_steps)
    def _experts():
        p = s * N_EXPERTS + j * ne
        riota = lax.broadcasted_iota(jnp.int32, (MOE_BLOCK, st), 0)
        nslots = [nslot_ref[p + i] for i in range(ne)]

        def slot(k, c):
            hits = [pos_ref[i] == riota + k * MOE_BLOCK for i in range(ne)]
            onehot = jnp.concatenate([jnp.where(hit, 1.0, 0.0).astype(BF16) for hit in hits], axis=0)
            x = jnp.dot(onehot, h2_ref[...], preferred_element_type=F32).astype(BF16).reshape(ne, MOE_BLOCK, -1)
            bmm = lambda a, w_ref: jnp.einsum('eij,ejk->eik', a, w_ref[...], preferred_element_type=F32)
            hg = bmm(x, wg_ref)
            y = bmm((hg * jax.nn.sigmoid(hg) * bmm(x, wu_ref)).astype(BF16), wdn_ref)
            for i in range(ne):
                @pl.when(k < nslots[i])
                def _store():
                    wcol = jnp.sum(jnp.where(hits[i], wd_ref[i], 0.0), axis=1, keepdims=True)
                    row0 = pl.multiple_of((bstart_ref[p + i] + k) * MOE_BLOCK, MOE_BLOCK)
                    yw_ref[pl.ds(row0, MOE_BLOCK), :] = (y[i] * wcol).astype(BF16)
            return c

        n_slots = nslots[0]
        for i in range(1, ne):
            n_slots = jnp.maximum(n_slots, nslots[i])
        lax.fori_loop(0, n_slots, slot, 0)

    @pl.when(j == n_expert_steps)
    def _combine():
        pt = rankt_ref[...]
        lane_e = lax.broadcasted_iota(jnp.int32, (st, N_EXPERTS), 1)
        li = lax.broadcasted_iota(jnp.int32, (st, MOE_BLOCK), 1)
        rows = MOE_CHUNK_BLOCKS * MOE_BLOCK

        def chunk(c, acc):
            hts = []
            for bi in range(MOE_CHUNK_BLOCKS):
                b = s * maxblk + c * MOE_CHUNK_BLOCKS + bi
                col = jnp.sum(jnp.where(lane_e == blke_ref[b], pt, 0.0), axis=1, keepdims=True)
                hts.append(jnp.where(col == (li + blkr_ref[b]).astype(F32), 1.0, 0.0).astype(BF16))
            ht = jnp.concatenate(hts, axis=1)
            return acc + jnp.dot(ht, yw_ref[pl.ds(pl.multiple_of(c * rows, rows), rows), :],
                                 preferred_element_type=F32)

        n_chunks = (nblk_ref[s] + MOE_CHUNK_BLOCKS - 1) // MOE_CHUNK_BLOCKS
        out_ref[...] = lax.fori_loop(0, n_chunks, chunk, jnp.zeros(out_ref.shape, F32))


def _moe2_routed(h2, wd, pos, rankt, cum, wg, wu, wdn):
    N, D = h2.shape
    E = N_EXPERTS
    n_tiles = N // TM
    tps = min(MOE_SUPER_TILES, n_tiles)
    ns = -(-n_tiles // tps)
    st = tps * TM
    pad = ns * st - N
    if pad:
        h2 = jnp.pad(h2, ((0, pad), (0, 0)))
        wd = jnp.pad(wd, ((0, 0), (0, pad)))
        pos = jnp.pad(pos, ((0, 0), (0, pad)), constant_values=-1)
        rankt = jnp.pad(rankt, ((0, pad), (0, 0)), constant_values=-1.0)
    maxblk = _moe_max_blocks(tps)
    meta = _moe2_meta(cum, n_tiles, tps)
    ne = MOE_EXPERTS_PER_STEP
    nq = E // ne
    ex = lambda j: jnp.minimum(j, nq - 1)
    grid_spec = pltpu.PrefetchScalarGridSpec(
        num_scalar_prefetch=5,
        grid=(ns, nq + 1),
        in_specs=[pl.BlockSpec((st, D), lambda s, j, *_: (s, 0)),
                  pl.BlockSpec((ne, 1, st), lambda s, j, *_: (ex(j), 0, s)),
                  pl.BlockSpec((st, E), lambda s, j, *_: (s, 0)),
                  pl.BlockSpec((ne, 1, st), lambda s, j, *_: (ex(j), 0, s)),
                  pl.BlockSpec((ne, D, D_EXPERT), lambda s, j, *_: (ex(j), 0, 0)),
                  pl.BlockSpec((ne, D, D_EXPERT), lambda s, j, *_: (ex(j), 0, 0)),
                  pl.BlockSpec((ne, D_EXPERT, D), lambda s, j, *_: (ex(j), 0, 0))],
        out_specs=pl.BlockSpec((st, D), lambda s, j, *_: (s, 0)),
        scratch_shapes=[pltpu.VMEM((maxblk * MOE_BLOCK, D), BF16)],
    )
    return pl.pallas_call(
        functools.partial(_moe2_kernel, tps=tps, maxblk=maxblk),
        grid_spec=grid_spec,
        out_shape=jax.ShapeDtypeStruct((ns * st, D), F32),
        compiler_params=_cparams(("arbitrary", "arbitrary"), MOE_VMEM_LIMIT),
        name="moe_routed",
    )(*meta, h2, pos.reshape(E, 1, ns * st), rankt, wd.reshape(E, 1, ns * st), wg, wu, wdn)


def _ffn_out_kernel(x1_ref, mod_ref, h2_ref, routed_ref, sg_ref, su_ref, sd_ref, post_ref, o_ref):
    h2 = h2_ref[...]
    hg = jnp.dot(h2, sg_ref[...], preferred_element_type=F32)
    hu = jnp.dot(h2, su_ref[...], preferred_element_type=F32)
    f = jnp.dot((hg * jax.nn.sigmoid(hg) * hu).astype(BF16), sd_ref[...], preferred_element_type=F32) + routed_ref[...]
    o_ref[0] = x1_ref[0] + mod_ref[0, 5:6, :] * _rms(f, post_ref[...])


def _ffn_out(x1, modl, h2, routed, sg, su, sd, post, n_lat):
    B, Tn, D = x1.shape
    nt = Tn // TM
    nb = modl.shape[0] - 1

    def mod_map(b, t):
        return (jnp.where(t >= n_lat, nb, b), 0, 0)

    row = pl.BlockSpec((1, TM, D), lambda b, t: (b, t, 0))
    flat = pl.BlockSpec((TM, D), lambda b, t: (b * nt + t, 0))
    return pl.pallas_call(
        _ffn_out_kernel,
        grid=(B, nt),
        in_specs=[row, pl.BlockSpec((1, 6, D), mod_map), flat, flat,
                  _const_spec(sg.shape), _const_spec(su.shape), _const_spec(sd.shape), _const_spec((1, D))],
        out_specs=row,
        out_shape=jax.ShapeDtypeStruct((B, Tn, D), F32),
        compiler_params=_cparams(("parallel", "parallel")),
        name="ffn_out",
    )(x1, modl, h2, routed, sg, su, sd, post)


def kernel(x, c, ctx, c_ctx, ada_w, ada_b, mix_norm_pre, mix_norm_post, ffn_norm_pre, ffn_norm_post, w_in, w_out, swa_sink, gdn_conv_w, gdn_a_log, gdn_dt_bias, gdn_norm_w, mla_q_norm, mla_w_uq, mla_kv_norm, mla_w_ukv, router_w, router_bias, expert_w_gate, expert_w_up, expert_w_down, shared_w_gate, shared_w_up, shared_w_down):
    B, S, D = x.shape
    L = ctx.shape[1]
    assert L == TM and S % (2 * TM) == 0
    T = S + L
    n_lat = S // TM
    xx = jnp.concatenate([x, ctx], axis=1)
    tabs = _rope_tables(S, T)
    cvecs = jnp.concatenate([c, c_ctx[None], jnp.zeros((8 - B - 1, D), F32)], axis=0)
    mods = _modulation(cvecs, ada_w, ada_b).reshape(DEPTH, 8, 6, D)[:, :B + 1]
    for layer in range(DEPTH):
        xx = _layer(layer, xx, mods[layer], tabs, S, layer == DEPTH - 1,
                    mix_norm_pre, mix_norm_post, ffn_norm_pre, ffn_norm_post, w_in, w_out, swa_sink, gdn_conv_w,
                    gdn_a_log, gdn_dt_bias, gdn_norm_w, mla_q_norm, mla_w_uq, mla_kv_norm, mla_w_ukv, router_w,
                    router_bias, expert_w_gate, expert_w_up, expert_w_down, shared_w_gate, shared_w_up, shared_w_down)
    return xx


def _layer(layer, xx, modl, tabs, S, last, mix_norm_pre, mix_norm_post, ffn_norm_pre, ffn_norm_post, w_in, w_out,
           swa_sink, gdn_conv_w, gdn_a_log, gdn_dt_bias, gdn_norm_w, mla_q_norm, mla_w_uq, mla_kv_norm, mla_w_ukv,
           router_w, router_bias, expert_w_gate, expert_w_up, expert_w_down, shared_w_gate, shared_w_up, shared_w_down):
    B, T, D = xx.shape
    n_lat = S // TM
    nt = n_lat if last else T // TM
    qa, ka, va, gx, z, ba, qm, km, vm = _project(
        xx, modl, mix_norm_pre[layer][None], _prep_w_in(w_in[layer]), _prep_w_uq(mla_w_uq[layer]),
        _prep_w_ukv(mla_w_ukv[layer]), jnp.pad(mla_q_norm[layer], (0, 256 - MLA_Q_RANK))[None],
        mla_kv_norm[layer][None], tabs, n_lat)
    oa = _swa_attention(swa_sink[layer], qa, ka, va, S, nt)
    oc = _mla_attention(qm, km, vm, S, not last)
    gp = jnp.zeros((2, LANE), F32)
    g0 = _BA_LANE + 2 * GDN_HEADS
    gp = gp.at[0, g0:g0 + 2 * GDN_HEADS].set(gdn_a_log[layer].reshape(-1))
    gp = gp.at[1, g0:g0 + 2 * GDN_HEADS].set(gdn_dt_bias[layer].reshape(-1))
    p1, p2, p3, gt = _gdn_local(gx, gdn_conv_w[layer], ba, gp, n_lat)
    o_f, o_b = _gdn_scan(p1, p2, p3, gt, S)

    x1, h2, wd, pos, rankt, cum = _mix_out(
        xx, modl, oa, o_f, o_b, z, oc, gdn_norm_w[layer][None], _prep_w_out(w_out[layer]),
        mix_norm_post[layer][None], ffn_norm_pre[layer][None], router_w[layer].T.astype(BF16),
        router_bias[layer][:, None], n_lat, nt, min(MOE_SUPER_TILES, B * nt))
    routed = _moe2_routed(h2, wd, pos, rankt, cum, expert_w_gate[layer].astype(BF16),
                          expert_w_up[layer].astype(BF16), expert_w_down[layer].astype(BF16))
    return _ffn_out(x1, modl, h2, routed, shared_w_gate[layer].astype(BF16), shared_w_up[layer].astype(BF16),
                    shared_w_down[layer].astype(BF16), ffn_norm_post[layer][None], n_lat)
```

```python
import functools
import math

import numpy as np
import jax
import jax.numpy as jnp
from jax import lax
from jax.experimental import pallas as pl
from jax.experimental.pallas import tpu as pltpu

F32 = jnp.float32
BF16 = jnp.bfloat16

DEPTH = 2
GRID_W = 64
EPS = 1e-6
ROPE_BASE = 10000.0
HEAD_DIM = 64
SWA_HEADS = 6
SWA_KV_HEADS = 2
SWA_WINDOW = 128
GDN_HEADS = 6
GDN_DK = 64
GDN_DV = 64
GDN_CONV = 5
GDN_CHUNK = 64
MLA_HEADS = 4
MLA_Q_RANK = 192
MLA_KV_RANK = 128
MLA_NOPE = 64
MLA_ROPE = 32
MLA_V = 64
N_EXPERTS = 64
MOE_TOP_K = 8
MOE_GROUPS = 8
MOE_TOPK_GROUPS = 4
D_EXPERT = 256
ROUTED_SCALE = 2.5
LOG2_E = math.log2(math.e)

_SPLITS = (384, 128, 128, 1152, 384, 24, 192, 128, 32)
_OFF = np.concatenate([[0], np.cumsum(_SPLITS)]).tolist()
D_PROJ = _OFF[-1]

TM = 256
LANE = 128
VMEM_LIMIT = 56 * 1024 * 1024

_C_QA, _C_QAS, _C_KA, _C_KAS, _C_VA, _C_GDN, _C_Z, _C_CQ, _C_CKV, _C_KRB, _C_END = (
    0, 768, 1536, 1664, 1792, 1920, 3072, 3456, 3712, 3840, 3968)
_BA_LANE = 32


def _cparams(sem, vmem=VMEM_LIMIT):
    return pltpu.CompilerParams(dimension_semantics=sem, vmem_limit_bytes=vmem)


def _rope_partner(d, width):
    half, n = width // 2, width // 4
    i = d % half
    return (d // half) * half + (i + n if i < n else i - n)


def _take_cols(w, idx):
    idx = [int(i) for i in idx]
    pieces, start = [], 0
    for pos in range(1, len(idx) + 1):
        run_ends = pos == len(idx) or (idx[pos] != idx[pos - 1] + 1 if idx[pos - 1] >= 0 else idx[pos] >= 0) \
            or (idx[pos] < 0) != (idx[pos - 1] < 0)
        if run_ends:
            n = pos - start
            pieces.append(jnp.zeros((w.shape[0], n), w.dtype) if idx[start] < 0 else w[:, idx[start]:idx[start] + n])
            start = pos
    return jnp.concatenate(pieces, axis=1)


def _prep_w_in(w_in):
    cols = []
    for swap in (False, True):
        for h in range(SWA_HEADS):
            j = h // (SWA_HEADS // SWA_KV_HEADS)
            blk = [-1] * LANE
            for d in range(HEAD_DIM):
                blk[64 * j + d] = _OFF[0] + h * HEAD_DIM + (_rope_partner(d, HEAD_DIM) if swap else d)
            cols += blk
    for swap in (False, True):
        for j in range(SWA_KV_HEADS):
            cols += [_OFF[1] + j * HEAD_DIM + (_rope_partner(d, HEAD_DIM) if swap else d) for d in range(HEAD_DIM)]
    cols += list(range(_OFF[2], _OFF[3]))
    cols += list(range(_OFF[3], _OFF[4]))
    cols += list(range(_OFF[4], _OFF[5]))
    cols += list(range(_OFF[6], _OFF[7])) + [-1] * 64
    cols += list(range(_OFF[7], _OFF[8]))
    cols += list(range(_OFF[8], _OFF[9])) + list(range(_OFF[5], _OFF[6])) + [-1] * (LANE - 32 - 24)
    assert len(cols) == _C_END
    return _take_cols(w_in, cols).astype(BF16)


def _prep_w_uq(w_uq):
    cols = []
    for swap in (False, True):
        for h in range(MLA_HEADS):
            base = h * (MLA_NOPE + MLA_ROPE)
            blk = [-1] * LANE
            for d in range(MLA_NOPE):
                blk[d] = -1 if swap else base + d
            for r in range(MLA_ROPE):
                blk[MLA_NOPE + r] = base + MLA_NOPE + (_rope_partner(r, MLA_ROPE) if swap else r)
            cols += blk
    w = _take_cols(w_uq, cols)
    return jnp.pad(w, ((0, 256 - MLA_Q_RANK), (0, 0))).astype(BF16)


def _prep_w_ukv(w_ukv):
    kcols, vcols = [], []
    for h in range(MLA_HEADS):
        base = h * (MLA_NOPE + MLA_V)
        kcols += [base + d for d in range(MLA_NOPE)] + [-1] * 64
        vcols += [base + MLA_NOPE + d for d in range(MLA_V)]
    wk = _take_cols(w_ukv, kcols)
    wv = _take_cols(w_ukv, vcols)
    top = jnp.concatenate([wk, jnp.zeros_like(wk), wv], axis=1)
    place = np.zeros((128, 1280), np.float32)
    for h in range(MLA_HEADS):
        for r in range(MLA_ROPE):
            place[r, 128 * h + MLA_NOPE + r] = 1.0
            place[_rope_partner(r, MLA_ROPE), 512 + 128 * h + MLA_NOPE + r] = 1.0
    return jnp.concatenate([top, jnp.asarray(place)], axis=0).astype(BF16)


def _prep_w_out(w_out):
    rows = []
    G = SWA_HEADS // SWA_KV_HEADS
    for g in range(G):
        for j in range(SWA_KV_HEADS):
            rows += [(G * j + g) * HEAD_DIM + d for d in range(HEAD_DIM)]
    rows += list(range(SWA_HEADS * HEAD_DIM, w_out.shape[0]))
    return jnp.take(w_out, jnp.asarray(rows), axis=0).astype(BF16)


def _rope_tables(S, T):
    t = np.arange(S)
    row, col = t // GRID_W, t % GRID_W

    def tab(width, lanes):
        half, n = width // 2, width // 4
        c = np.ones((T, LANE), np.float64)
        s = np.zeros((T, LANE), np.float64)
        for lane, d in lanes:
            i = d % half
            pos = row if d < half else col
            ang =(pos.astype(np.float32) * np.float32(ROPE_BASE ** (-(i % n) / n))).astype(np.float64)
            c[:S, lane] = np.cos(ang)
            s[:S, lane] = -np.sin(ang) if i < n else np.sin(ang)
        return jnp.asarray(c, F32), jnp.asarray(s, F32)

    ca, sa = tab(HEAD_DIM, [(l, l % HEAD_DIM) for l in range(LANE)])
    cm, sm = tab(MLA_ROPE, [(MLA_NOPE + r, r) for r in range(MLA_ROPE)])
    return ca, sa, cm, sm


def _mod_kernel(c_ref, w_ref, b_ref, o_ref):
    cv = c_ref[...]
    a = (cv * jax.nn.sigmoid(cv)).astype(BF16)
    o_ref[0] = jnp.dot(a, w_ref[0].astype(BF16), preferred_element_type=F32) + b_ref[0]


def _modulation(cvecs, ada_w, ada_b):
    depth, D, N = ada_w.shape
    tn = 512
    return pl.pallas_call(
        _mod_kernel,
        grid=(depth, N // tn),
        in_specs=[pl.BlockSpec((8, D), lambda l, j: (0, 0)),
                  pl.BlockSpec((1, D, tn), lambda l, j: (l, 0, j)),
                  pl.BlockSpec((1, 1, tn), lambda l, j: (l, 0, j))],
        out_specs=pl.BlockSpec((1, 8, tn), lambda l, j: (l, 0, j)),
        out_shape=jax.ShapeDtypeStruct((depth, 8, N), F32),
        compiler_params=_cparams(("arbitrary", "arbitrary")),
        name="modulation",
    )(cvecs, ada_w, ada_b.reshape(depth, 1, N))


def _proj_kernel(x_ref, mod_ref, gain_ref, w_ref, wq2_ref, wk2_ref, qg_ref, kvg_ref,
                 ca_ref, sa_ref, cm_ref, sm_ref,
                 qa_ref, ka_ref, va_ref, gx_ref, z_ref, ba_ref, qm_ref, km_ref, vm_ref):
    x = x_ref[0]
    ms = jnp.mean(x * x, axis=-1, keepdims=True)
    h = x * lax.rsqrt(ms + EPS) * gain_ref[...]
    h = h * (1.0 + mod_ref[0, 1:2, :]) + mod_ref[0, 0:1, :]
    hb = h.astype(BF16)

    ca, sa, cm, sm = ca_ref[...], sa_ref[...], cm_ref[...], sm_ref[...]

    def rope(a, b, c, s):
        n = a.shape[1] // LANE
        return a * jnp.concatenate([c] * n, axis=1) + b * jnp.concatenate([s] * n, axis=1)

    p1 = jnp.dot(hb, w_ref[:, _C_QA:_C_GDN], preferred_element_type=F32)
    qa = rope(p1[:, _C_QA:_C_QAS], p1[:, _C_QAS:_C_KA], ca, sa)
    qa_ref[0] = (qa * (HEAD_DIM ** -0.5)).astype(BF16)
    ka_ref[0] = rope(p1[:, _C_KA:_C_KAS], p1[:, _C_KAS:_C_VA], ca, sa).astype(BF16)
    va_ref[0] = p1[:, _C_VA:_C_GDN].astype(BF16)

    p2 = jnp.dot(hb, w_ref[:, _C_GDN:_C_Z], preferred_element_type=F32)
    for j in range(3 * GDN_HEADS):
        gx_ref[0, j] = p2[:, 64 * j:64 * j + 64]

    p3 = jnp.dot(hb, w_ref[:, _C_Z:_C_END], preferred_element_type=F32)
    z_ref[0] = p3[:, 0:_C_CQ - _C_Z]
    cq = p3[:, _C_CQ - _C_Z:_C_CKV - _C_Z]
    ckv = p3[:, _C_CKV - _C_Z:_C_KRB - _C_Z]
    krb = p3[:, _C_KRB - _C_Z:]
    ba_ref[0] = krb

    cqn = cq * lax.rsqrt(jnp.sum(cq * cq, axis=-1, keepdims=True) * (1.0 / MLA_Q_RANK) + EPS) * qg_ref[...]
    e = jnp.dot(cqn.astype(BF16), wq2_ref[...], preferred_element_type=F32)
    qm = rope(e[:, :512], e[:, 512:], cm, sm)
    qm_ref[0] = (qm * (LOG2_E * (MLA_NOPE + MLA_ROPE) ** -0.5)).astype(BF16)

    ckvn = ckv * lax.rsqrt(jnp.mean(ckv * ckv, axis=-1, keepdims=True) + EPS) * kvg_ref[...]
    lhs2 = jnp.concatenate([ckvn.astype(BF16), krb.astype(BF16)], axis=1)
    e2 = jnp.dot(lhs2, wk2_ref[...], preferred_element_type=F32)
    km_ref[0] = rope(e2[:, :512], e2[:, 512:1024], cm, sm).astype(BF16)
    vm_ref[0] = e2[:, 1024:].astype(BF16)


def _const_spec(shape):
    nd = len(shape)
    return pl.BlockSpec(shape, lambda *_: (0,) * nd)


def _project(xx, modl, gain, w_main, wq2, wk2, qg, kvg, tabs, n_lat_tiles):
    B, T, D = xx.shape
    nt = T // TM
    nb = modl.shape[0] - 1

    def mod_map(b, t):
        return (jnp.where(t >= n_lat_tiles, nb, b), 0, 0)

    row = lambda w: pl.BlockSpec((1, TM, w), lambda b, t: (b, t, 0))
    tab = pl.BlockSpec((TM, LANE), lambda b, t: (t, 0))
    out_shapes = [
        jax.ShapeDtypeStruct((B, T, 768), BF16),
        jax.ShapeDtypeStruct((B, T, 128), BF16),
        jax.ShapeDtypeStruct((B, T, 128), BF16),
        jax.ShapeDtypeStruct((B, 18, T, 64), F32),
        jax.ShapeDtypeStruct((B, T, 384), F32),
        jax.ShapeDtypeStruct((B, T, 128), F32),
        jax.ShapeDtypeStruct((B, T, 512), BF16),
        jax.ShapeDtypeStruct((B, T, 512), BF16),
        jax.ShapeDtypeStruct((B, T, 256), BF16),
    ]
    out_specs = [row(768), row(128), row(128),
                 pl.BlockSpec((1, 18, TM, 64), lambda b, t: (b, 0, t, 0)),
                 row(384), row(128), row(512), row(512), row(256)]
    return pl.pallas_call(
        _proj_kernel,
        grid=(B, nt),
        in_specs=[row(D), pl.BlockSpec((1, 6, D), mod_map), _const_spec((1, D)),
                  _const_spec(w_main.shape), _const_spec(wq2.shape), _const_spec(wk2.shape),
                  _const_spec((1, 256)), _const_spec((1, 128)), tab, tab, tab, tab],
        out_specs=out_specs,
        out_shape=out_shapes,
        compiler_params=_cparams(("parallel", "parallel")),
        name="in_proj",
    )(xx, modl, gain, w_main, wq2, wk2, qg, kvg, *tabs)


def _nt_dot(a, b):
    return lax.dot_general(a, b, (((1,), (1,)), ((), ())), preferred_element_type=F32)


def _swa_kernel(sink_ref, q_ref, k_ref, v_ref, o_ref, *, S, n_lat):
    i = pl.program_id(1)
    G = SWA_HEADS // SWA_KV_HEADS
    W = 2 * TM
    lane = lax.broadcasted_iota(jnp.int32, (TM, LANE), 1)
    kc = k_ref[0, pl.ds(S, TM), :]
    vc = v_ref[0, pl.ds(S, TM), :]

    def heads(local):
        outs = []
        for g in range(G):
            og = []
            for j in range(SWA_KV_HEADS):
                h = G * j + g
                q = q_ref[0, :, LANE * h:LANE * (h + 1)]
                sink = sink_ref[h]
                s_ctx = _nt_dot(q, kc)
                m = jnp.maximum(jnp.max(s_ctx, axis=-1, keepdims=True), sink)
                if local is not None:
                    kw, vw, valid = local
                    s_loc = jnp.where(valid, _nt_dot(q, kw), -jnp.inf)
                    m = jnp.maximum(m, jnp.max(s_loc, axis=-1, keepdims=True))
                p_ctx = jnp.exp(s_ctx - m)
                den = jnp.sum(p_ctx, axis=-1, keepdims=True) + jnp.exp(sink - m)
                o = jnp.dot(p_ctx.astype(BF16), vc, preferred_element_type=F32)
                if local is not None:
                    p_loc = jnp.exp(s_loc - m)
                    den = den + jnp.sum(p_loc, axis=-1, keepdims=True)
                    o = o + jnp.dot(p_loc.astype(BF16), vw, preferred_element_type=F32)
                og.append(o * (1.0 / den))
            outs.append(jnp.where(lane < HEAD_DIM, og[0], og[1]))
        o_ref[0] = jnp.concatenate(outs, axis=1).astype(BF16)

    @pl.when(i < n_lat)
    def _latent():
        start = pl.multiple_of(jnp.clip(i * TM - SWA_WINDOW, 0, S - W), LANE)
        kw = k_ref[0, pl.ds(start, W), :]
        vw = v_ref[0, pl.ds(start, W), :]
        qpos = i * TM + lax.broadcasted_iota(jnp.int32, (TM, W), 0)
        kpos = start + lax.broadcasted_iota(jnp.int32, (TM, W), 1)
        heads((kw, vw, jnp.abs(qpos - kpos) <= SWA_WINDOW))

    @pl.when(i >= n_lat)
    def _context():
        heads(None)


def _swa_attention(sink, qa, ka, va, S, nt):
    B, T, _ = qa.shape
    kern = functools.partial(_swa_kernel, S=S, n_lat=S // TM)
    return pl.pallas_call(
        kern,
        grid=(B, nt),
        in_specs=[pl.BlockSpec(memory_space=pltpu.SMEM),
                  pl.BlockSpec((1, TM, 768), lambda b, t: (b, t, 0)),
                  pl.BlockSpec((1, T, 128), lambda b, t: (b, 0, 0)),
                  pl.BlockSpec((1, T, 128), lambda b, t: (b, 0, 0))],
        out_specs=pl.BlockSpec((1, TM, 384), lambda b, t: (b, t, 0)),
        out_shape=jax.ShapeDtypeStruct((B, nt * TM, 384), BF16),
        compiler_params=_cparams(("parallel", "parallel")),
        name="swa_attention",
    )(sink, qa, ka, va)


MLA_SUBSTEPS = 2
MLA_KEY_CHUNK = 4608


def _mla_kernel(q_ref, k_ref, v_ref, o_ref, *, S, tq, latent):
    lane = lax.broadcasted_iota(jnp.int32, (tq, LANE), 1)
    n_keys = S + TM
    n_chunks = min(n for n in range(1, 65) if n_keys % (n * LANE) == 0 and n_keys // n <= MLA_KEY_CHUNK) if latent else 0
    chunk = n_keys // n_chunks if latent else 0
    blocks_per_sub = chunk // LANE // MLA_SUBSTEPS
    sub_sizes = [blocks_per_sub * LANE] * (MLA_SUBSTEPS - 1) + [chunk - blocks_per_sub * LANE * (MLA_SUBSTEPS - 1)]

    outs = []
    for h in range(MLA_HEADS):
        q = q_ref[0, :, LANE * h:LANE * (h + 1)]

        def step(off, size, carry):
            m, l, acc = carry
            s = _nt_dot(q, k_ref[0, pl.ds(off, size), LANE * h:LANE * (h + 1)])
            blocks = [s[:, LANE * j:LANE * (j + 1)] for j in range(size // LANE)]
            mx = blocks[0]
            for blk in blocks[1:]:
                mx = jnp.maximum(mx, blk)
            m_new = jnp.maximum(m, jnp.broadcast_to(jnp.max(mx, axis=-1, keepdims=True), (tq, LANE)))
            alpha = jnp.exp2(m - m_new)
            ps = [jnp.exp2(blk - m_new) for blk in blocks]
            l = alpha * l
            for p in ps:
                l = l + p
            p = jnp.concatenate(ps, axis=1).astype(BF16)
            v = v_ref[0, pl.ds(off, size), LANE * (h // 2):LANE * (h // 2 + 1)]
            return m_new, l, alpha * acc + jnp.dot(p, v, preferred_element_type=F32)

        def body(c, cr):
            off = pl.multiple_of(c * chunk, chunk)
            for u in range(MLA_SUBSTEPS):
                cr = step(pl.multiple_of(off + sum(sub_sizes[:u]), LANE), sub_sizes[u], cr)
            return cr

        zero = jnp.zeros((tq, LANE), F32)
        carry = (jnp.full((tq, LANE), -jnp.inf, F32), zero, zero)
        m, l, acc = lax.fori_loop(0, n_chunks, body, carry) if latent else step(S, TM, carry)
        outs.append(acc * (1.0 / jnp.sum(l, axis=-1, keepdims=True)))
    o_ref[0] = jnp.concatenate([jnp.where(lane < MLA_V, outs[0], outs[1]),
                                jnp.where(lane < MLA_V, outs[2], outs[3])], axis=1).astype(BF16)


MLA_Q_TILE = 256


def _mla_attention(qm, km, vm, S, with_context):
    B, T, _ = qm.shape

    def call(tq, n_q, q_block0, latent):
        return pl.pallas_call(
            functools.partial(_mla_kernel, S=S, tq=tq, latent=latent),
            grid=(B, n_q),
            in_specs=[pl.BlockSpec((1, tq, 512), lambda b, t: (b, q_block0 + t, 0)),
                      pl.BlockSpec((1, T, 512), lambda b, t: (b, 0, 0)),
                      pl.BlockSpec((1, T, 256), lambda b, t: (b, 0, 0))],
            out_specs=pl.BlockSpec((1, tq, 256), lambda b, t: (b, t, 0)),
            out_shape=jax.ShapeDtypeStruct((B, n_q * tq, 256), BF16),
            compiler_params=_cparams(("parallel", "parallel")),
            name="mla_attention" if latent else "mla_context",
        )(qm, km, vm)

    out = call(MLA_Q_TILE, S // MLA_Q_TILE, 0, True)
    if with_context:
        out = jnp.concatenate([out, call(TM, 1, S // TM, False)], axis=1)
    return out


CH = GDN_CHUNK
NCH = TM // CH


def _split3(a):
    hi = a.astype(BF16)
    r = a - hi.astype(F32)
    mid = r.astype(BF16)
    return hi, mid, (r - mid.astype(F32)).astype(BF16)


def _bmm3(a, b):
    ah, al, _ = _split3(a)
    bh, bl, _ = _split3(b)
    lhs = jnp.concatenate([ah, al, ah], axis=-1)
    rhs = jnp.concatenate([bh, bh, bl], axis=1)
    return jnp.einsum('cij,cjk->cik', lhs, rhs, preferred_element_type=F32)


GDN_HEADS_PER_STEP = 3


def _gdn_local_kernel(x_ref, xp_ref, xn_ref, cw_ref, ba_ref, gp_ref,
                      p1_ref, p2_ref, p3_ref, gt_ref, xs_ref, *, n_lat):
    t = pl.program_id(2)
    has_prev = jnp.logical_and(t > 0, t != n_lat)
    has_next = t < n_lat - 1

    baf = ba_ref[0]
    lane = lax.broadcasted_iota(jnp.int32, (TM, LANE), 1)
    beta_all = jax.nn.sigmoid(baf)
    xg = baf + gp_ref[1:2, :]
    g_all = -jnp.exp(gp_ref[0:1, :]) * (jnp.maximum(xg, 0.0) + jnp.log(1.0 + jnp.exp(-jnp.abs(xg))))

    def col(a, idx):
        cvec = jnp.sum(jnp.where(lane == idx, a, 0.0), axis=1, keepdims=True)
        return jnp.broadcast_to(cvec, (TM, CH)).reshape(NCH, CH, CH)

    ii = lax.broadcasted_iota(jnp.int32, (CH, CH), 0)
    jj = lax.broadcasted_iota(jnp.int32, (CH, CH), 1)
    eye = (ii == jj).astype(F32)[None]
    nt = lambda a, b: jnp.einsum('cid,cjd->cij', a, b, preferred_element_type=F32)

    groups = []
    for hh in range(GDN_HEADS_PER_STEP):
        h = pl.program_id(1) * GDN_HEADS_PER_STEP + hh
        parts = []
        for part in range(3):
            xs_ref[hh, part, 0:8, :] = jnp.where(has_prev, xp_ref[0, part, hh], 0.0)
            xs_ref[hh, part, 8:8 + TM, :] = x_ref[0, part, hh]
            xs_ref[hh, part, 8 + TM:16 + TM, :] = jnp.where(has_next, xn_ref[0, part, hh], 0.0)
            acc = jnp.zeros((TM, GDN_DK), F32)
            for k in range(GDN_CONV):
                acc = acc + cw_ref[hh, part, k:k + 1, :] * xs_ref[hh, part, pl.ds(8 - GDN_CONV // 2 + k, TM), :]
            parts.append(acc * jax.nn.sigmoid(acc))
        q_, k_, v = parts
        q = q_ * lax.rsqrt(jnp.sum(q_ * q_, axis=-1, keepdims=True) + EPS) * (GDN_DK ** -0.5)
        k = k_ * lax.rsqrt(jnp.sum(k_ * k_, axis=-1, keepdims=True) + EPS)
        q3, k3, v3 = (a.reshape(NCH, CH, GDN_DK) for a in (q, k, v))
        kb, qb = k3.astype(BF16), q3.astype(BF16)
        kk, qk = nt(kb, kb), nt(qb, kb)
        for d in range(2):
            groups.append(dict(
                q3=q3, k3=k3, v3=v3, kk=kk, qk=qk,
                beta=col(beta_all, _BA_LANE + GDN_HEADS * d + h),
                g=col(g_all, _BA_LANE + 2 * GDN_HEADS + GDN_HEADS * d + h)))

    ngrp = len(groups)
    stack = lambda key: jnp.concatenate([grp[key] for grp in groups], axis=0)
    per_dir = lambda lo_, up_: jnp.concatenate(
        [jnp.broadcast_to((lo_ if gi % 2 == 0 else up_)[None], (NCH, CH, CH)) for gi in range(ngrp)], axis=0)
    incl = per_dir(jj <= ii, jj >= ii)
    strict = per_dir(jj < ii, jj > ii)
    q3, k3, v3, kk, qk, beta, g = (stack(key) for key in ('q3', 'k3', 'v3', 'kk', 'qk', 'beta', 'g'))
    nb = ngrp * NCH
    tri = incl.astype(BF16)
    gcx = jnp.einsum('cij,cjl->cil', jnp.concatenate([tri] * 3, axis=-1),
                     jnp.concatenate(_split3(g), axis=1), preferred_element_type=F32)
    gcr = jnp.stack([gcx[c].T for c in range(nb)], axis=0)
    decay = jnp.exp(jnp.where(incl, gcx - gcr, -jnp.inf))
    lm = jnp.where(strict, beta * kk * decay, 0.0)
    egc = jnp.exp(gcx)
    rhs = jnp.concatenate([v3 * beta, k3 * beta * egc], axis=-1)
    p = -lm
    tinv = eye + p
    p = _bmm3(p, p)
    for _ in range(4):
        both = _bmm3(p, jnp.concatenate([tinv, p], axis=-1))
        tinv = tinv + both[..., :CH]
        p = both[..., CH:]
    tinv = tinv + _bmm3(p, tinv)
    uw = _bmm3(tinv, rhs)
    gce = jnp.concatenate(
        [gcx[NCH * gi:NCH * (gi + 1), (CH - 1 if gi % 2 == 0 else 0):(CH if gi % 2 == 0 else 1), :] for gi in range(ngrp)],
        axis=0)
    k_out = k3 * jnp.exp(gce - gcx)
    k_out_t = jnp.stack([k_out[c].T for c in range(nb)], axis=0)
    kq = jnp.concatenate([k_out_t, q3 * egc], axis=-1)
    intra = qk * decay
    gte = jnp.exp(gce)
    gtb = jnp.broadcast_to(jnp.concatenate([gte, gte], axis=-1), (nb, 8, LANE))
    for gi in range(ngrp):
        hh, d = gi // 2, gi % 2
        sl = slice(NCH * gi, NCH * (gi + 1))
        p1_ref[d, 0, hh] = uw[sl].reshape(TM, 2 * CH).astype(BF16)
        p2_ref[d, 0, hh] = kq[sl].reshape(TM, 2 * CH).astype(BF16)
        p3_ref[d, 0, hh] = intra[sl].reshape(TM, CH).astype(BF16)
        gt_ref[d, 0, hh] = gtb[sl].reshape(NCH * 8, LANE)


def _gdn_local(gx, conv_w, ba, gp, n_lat):
    B, _, T, _ = gx.shape
    H = GDN_HEADS
    hp = GDN_HEADS_PER_STEP
    nt = T // TM
    gx5 = gx.reshape(B, 3, H, T, GDN_DK)
    cw = jnp.transpose(conv_w.reshape(GDN_CONV, 3, H, GDN_DK), (2, 1, 0, 3))
    kern = functools.partial(_gdn_local_kernel, n_lat=n_lat)
    r8 = TM // 8
    big = lambda w, dt: jax.ShapeDtypeStruct((2, B, H, T, w), dt)
    ospec = lambda w: pl.BlockSpec((2, 1, hp, TM, w), lambda b, h, t: (0, b, h, t, 0))
    return pl.pallas_call(
        kern,
        grid=(B, H // hp, nt),
        in_specs=[pl.BlockSpec((1, 3, hp, TM, GDN_DK), lambda b, h, t: (b, 0, h, t, 0)),
                  pl.BlockSpec((1, 3, hp, 8, GDN_DK), lambda b, h, t: (b, 0, h, jnp.maximum(t * r8 - 1, 0), 0)),
                  pl.BlockSpec((1, 3, hp, 8, GDN_DK), lambda b, h, t: (b, 0, h, jnp.minimum((t + 1) * r8, T // 8 - 1), 0)),
                  pl.BlockSpec((hp, 3, GDN_CONV, GDN_DK), lambda b, h, t: (h, 0, 0, 0)),
                  pl.BlockSpec((1, TM, LANE), lambda b, h, t: (b, t, 0)),
                  pl.BlockSpec((2, LANE), lambda b, h, t: (0, 0))],
        out_specs=[ospec(2 * CH), ospec(2 * CH), ospec(CH),
                   pl.BlockSpec((2, 1, hp, NCH * 8, LANE), lambda b, h, t: (0, b, h, t, 0))],
        out_shape=[big(2 * CH, BF16), big(2 * CH, BF16), big(CH, BF16),
                   jax.ShapeDtypeStruct((2, B, H, (T // CH) * 8, LANE), F32)],
        scratch_shapes=[pltpu.VMEM((hp, 3, TM + 16, GDN_DK), F32)],
        compiler_params=_cparams(("parallel", "parallel", "parallel")),
        name="gdn_local",
    )(gx5, gx5, gx5, cw, ba, gp)


def _gdn_scan_kernel(p1f, p2f, p3f, gtf, p1b, p2b, p3b, gtb, of_ref, ob_ref, s_ref):
    @pl.when(pl.program_id(0) == 0)
    def _init():
        s_ref[...] = jnp.zeros_like(s_ref)

    nb, nh = s_ref.shape[1], s_ref.shape[2]
    n = nb * nh
    bmm = lambda a, b: jnp.einsum('nij,njk->nik', a, b, preferred_element_type=F32)
    new_states = []
    for d, (p1, p2, p3, gt, o_ref) in enumerate(((p1f, p2f, p3f, gtf, of_ref), (p1b, p2b, p3b, gtb, ob_ref))):
        s = s_ref[d].reshape(n, GDN_DK, GDN_DV)
        uw = p1[0].reshape(n, CH, 2 * CH)
        kq = p2[0].reshape(n, CH, 2 * CH)
        m1 = bmm(jnp.concatenate([uw[:, :, CH:], kq[:, :, CH:]], axis=1), s.astype(BF16))
        v_new = (uw[:, :, :CH].astype(F32) - m1[:, :CH]).astype(BF16)
        o = m1[:, CH:] + bmm(p3[0].reshape(n, CH, CH), v_new)
        o_ref[...] = o.reshape(nb, nh, CH, GDN_DV)
        g = gt[0].reshape(n, 8, LANE)[:, 0:1, :CH]
        new_states.append(s * g + bmm(kq[:, :, :CH], v_new))
    for d in range(2):
        s_ref[d] = new_states[d].reshape(nb, nh, GDN_DK, GDN_DV)


def _gdn_scan(p1, p2, p3, gt, S):
    _, B, H, T, _ = p1.shape
    n = T // CH
    n_lat = S // CH
    fwd = lambda i: (n_lat + i) % n
    bwd = lambda i: n - 1 - i
    specs = []
    for d, order in ((0, fwd), (1, bwd)):
        for w, rows in ((2 * CH, CH), (2 * CH, CH), (CH, CH), (LANE, 8)):
            specs.append(pl.BlockSpec((1, B, H, rows, w), functools.partial(lambda i, d, order: (d, 0, 0, order(i), 0), d=d, order=order)))
    out = jax.ShapeDtypeStruct((B, H, T, GDN_DV), F32)
    return pl.pallas_call(
        _gdn_scan_kernel,
        grid=(n,),
        in_specs=specs,
        out_specs=[pl.BlockSpec((B, H, CH, GDN_DV), lambda i: (0, 0, fwd(i), 0)),
                   pl.BlockSpec((B, H, CH, GDN_DV), lambda i: (0, 0, bwd(i), 0))],
        out_shape=[out, out],
        scratch_shapes=[pltpu.VMEM((2, B, H, GDN_DK, GDN_DV), F32)],
        compiler_params=_cparams(("arbitrary",)),
        name="gdn_scan",
    )(p1, p2, p3, gt, p1, p2, p3, gt)


def _rms(v, gain):
    return v * lax.rsqrt(jnp.mean(v * v, axis=-1, keepdims=True) + EPS) * gain


def _mix_out_kernel(x_ref, mod_ref, oa_ref, of_ref, ob_ref, z_ref, oc_ref, gw_ref, wo_ref,
                    post_ref, pre2_ref, rw_ref, rb_ref,
                    x1_ref, h2_ref, wd_ref, pos_ref, rankt_ref, cum_ref, carry_ref, *, tiles_per_super):
    g = pl.program_id(0) * pl.num_programs(1) + pl.program_id(1)

    @pl.when(g % tiles_per_super == 0)
    def _reset():
        carry_ref[...] = jnp.zeros_like(carry_ref)

    z = z_ref[0]
    gated = []
    for h in range(GDN_HEADS):
        o = of_ref[0, h] + ob_ref[0, h]
        zh = z[:, GDN_DV * h:GDN_DV * (h + 1)]
        gated.append(_rms(o, gw_ref[...]) * (zh * jax.nn.sigmoid(zh)))
    mixed = jnp.concatenate([oa_ref[0], jnp.concatenate(gated, axis=1).astype(BF16), oc_ref[0]], axis=1)
    y = jnp.dot(mixed, wo_ref[...], preferred_element_type=F32)
    x1 = x_ref[0] + mod_ref[0, 2:3, :] * _rms(y, post_ref[...])
    x1_ref[0] = x1
    h2 = (_rms(x1, pre2_ref[...]) * (1.0 + mod_ref[0, 4:5, :]) + mod_ref[0, 3:4, :]).astype(BF16)
    h2_ref[...] = h2

    scores = jax.nn.sigmoid(_nt_dot(rw_ref[...], h2))
    sel = scores + rb_ref[...]
    gsz = N_EXPERTS // MOE_GROUPS
    g3 = sel.reshape(MOE_GROUPS, gsz, TM)
    io = lax.broadcasted_iota(jnp.int32, g3.shape, 1)
    m1 = jnp.max(g3, axis=1, keepdims=True)
    i1 = jnp.min(jnp.where(g3 == m1, io, gsz), axis=1, keepdims=True)
    m2 = jnp.max(jnp.where(io == i1, -jnp.inf, g3), axis=1, keepdims=True)

    def top_mask(vals, k):
        n = vals.shape[0]
        idx = lax.broadcasted_iota(jnp.int32, vals.shape, 0)
        mask = jnp.zeros(vals.shape, F32)
        for _ in range(k):
            mx = jnp.max(vals, axis=0, keepdims=True)
            hit = idx == jnp.min(jnp.where(vals == mx, idx, n), axis=0, keepdims=True)
            mask = jnp.where(hit, 1.0, mask)
            vals = jnp.where(hit, -jnp.inf, vals)
        return mask

    gmask = top_mask((m1 + m2).reshape(MOE_GROUPS, TM), MOE_TOPK_GROUPS)
    masked = jnp.where(gmask.reshape(MOE_GROUPS, 1, TM) > 0.0, g3, -jnp.inf).reshape(N_EXPERTS, TM)
    smask = top_mask(masked, MOE_TOP_K)
    w = jnp.where(smask > 0.0, scores, 0.0)
    wn = w / jnp.sum(w, axis=0, keepdims=True) * ROUTED_SCALE
    wd_ref[...] = jnp.where(smask > 0.0, wn, -1.0)

    ci = lax.broadcasted_iota(jnp.int32, (TM, TM), 0)
    cj = lax.broadcasted_iota(jnp.int32, (TM, TM), 1)
    ut = jnp.where(ci <= cj, 1.0, 0.0).astype(BF16)
    cs = jnp.dot(smask.astype(BF16), ut, preferred_element_type=F32)
    carry = carry_ref[...]
    posf = jnp.where(smask > 0.0, carry + cs - 1.0, -1.0)
    pos_ref[...] = posf.astype(jnp.int32)
    rankt_ref[...] = posf.T
    carry = carry + jnp.broadcast_to(cs[:, TM - 1:TM], (N_EXPERTS, TM))
    carry_ref[...] = carry
    cum_ref[0] = carry[:, :LANE]


def _mix_out(xx, modl, oa, o_f, o_b, z, oc, gw, wo, post, pre2, rw, rb, n_lat, nt, tiles_per_super):
    B, T, D = xx.shape
    nb = modl.shape[0] - 1
    N = B * nt * TM

    def mod_map(b, t):
        return (jnp.where(t >= n_lat, nb, b), 0, 0)

    row = lambda w: pl.BlockSpec((1, TM, w), lambda b, t: (b, t, 0))
    flat = lambda b, t: (0, b * nt + t)
    kern = functools.partial(_mix_out_kernel, tiles_per_super=tiles_per_super)
    return pl.pallas_call(
        kern,
        grid=(B, nt),
        in_specs=[row(D), pl.BlockSpec((1, 6, D), mod_map), row(384),
                  pl.BlockSpec((1, GDN_HEADS, TM, GDN_DV), lambda b, t: (b, 0, t, 0)),
                  pl.BlockSpec((1, GDN_HEADS, TM, GDN_DV), lambda b, t: (b, 0, t, 0)),
                  row(384), row(256), _const_spec((1, GDN_DV)), _const_spec(wo.shape),
                  _const_spec((1, D)), _const_spec((1, D)), _const_spec(rw.shape), _const_spec((N_EXPERTS, 1))],
        out_specs=[row(D),
                   pl.BlockSpec((TM, D), lambda b, t: (b * nt + t, 0)),
                   pl.BlockSpec((N_EXPERTS, TM), flat),
                   pl.BlockSpec((N_EXPERTS, TM), flat),
                   pl.BlockSpec((TM, N_EXPERTS), lambda b, t: (b * nt + t, 0)),
                   pl.BlockSpec((1, N_EXPERTS, LANE), lambda b, t: (b * nt + t, 0, 0))],
        out_shape=[jax.ShapeDtypeStruct((B, nt * TM, D), F32),
                   jax.ShapeDtypeStruct((N, D), BF16),
                   jax.ShapeDtypeStruct((N_EXPERTS, N), F32),
                   jax.ShapeDtypeStruct((N_EXPERTS, N), jnp.int32),
                   jax.ShapeDtypeStruct((N, N_EXPERTS), F32),
                   jax.ShapeDtypeStruct((N // TM, N_EXPERTS, LANE), F32)],
        scratch_shapes=[pltpu.VMEM((N_EXPERTS, TM), F32)],
        compiler_params=_cparams(("arbitrary", "arbitrary")),
        name="mix_out_router",
    )(xx, modl, oa, o_f, o_b, z, oc, gw, wo, post, pre2, rw, rb)


MOE_BLOCK = 128


MOE_SUPER_TILES = 3
MOE_CHUNK_BLOCKS = 16
MOE_EXPERTS_PER_STEP = 4
MOE_NO_RANK = 1 << 20
MOE_VMEM_LIMIT = 62 * 1024 * 1024


def _moe_max_blocks(tps):
    b = tps * TM * MOE_TOP_K // MOE_BLOCK + N_EXPERTS
    return -(-b // MOE_CHUNK_BLOCKS) * MOE_CHUNK_BLOCKS


def _moe2_meta(cum, n_tiles, tps):
    ns = -(-n_tiles // tps)
    maxblk = _moe_max_blocks(tps)
    last = jnp.minimum((jnp.arange(ns) + 1) * tps - 1, n_tiles - 1)
    cnt = cum[last, :, 0].astype(jnp.int32)
    nslot = (cnt + MOE_BLOCK - 1) // MOE_BLOCK
    bend = jnp.cumsum(nslot, axis=1)
    bstart = bend - nslot
    b = jnp.arange(maxblk, dtype=jnp.int32)
    blk_e = jnp.minimum(jnp.sum(bend[:, None, :] <= b[None, :, None], axis=2), N_EXPERTS - 1)
    blk_k = b[None, :] - jnp.take_along_axis(bstart, blk_e, axis=1)
    blk_r0 = jnp.where(b[None, :] < bend[:, -1:], blk_k * MOE_BLOCK, MOE_NO_RANK)
    i32 = lambda a: a.reshape(-1).astype(jnp.int32)
    return i32(nslot), i32(bstart), i32(bend[:, -1]), i32(blk_e), i32(blk_r0)


def _moe2_kernel(nslot_ref, bstart_ref, nblk_ref, blke_ref, blkr_ref,
                 h2_ref, pos_ref, rankt_ref, wd_ref, wg_ref, wu_ref, wdn_ref, out_ref, yw_ref,
                 *, tps, maxblk):
    s = pl.program_id(0)
    j = pl.program_id(1)
    st = tps * TM

    @pl.when(jnp.logical_and(s == 0, j == 0))
    def _init():
        yw_ref[...] = jnp.zeros_like(yw_ref)

    ne = MOE_EXPERTS_PER_STEP
    n_expert_steps = N_EXPERTS // ne

    @pl.when(j < n_expert_steps)
    def _experts():
        p = s * N_EXPERTS + j * ne
        riota = lax.broadcasted_iota(jnp.int32, (MOE_BLOCK, st), 0)
        nslots = [nslot_ref[p + i] for i in range(ne)]

        def slot(k, c):
            hits = [pos_ref[i] == riota + k * MOE_BLOCK for i in range(ne)]
            onehot = jnp.concatenate([jnp.where(hit, 1.0, 0.0).astype(BF16) for hit in hits], axis=0)
            x = jnp.dot(onehot, h2_ref[...], preferred_element_type=F32).astype(BF16).reshape(ne, MOE_BLOCK, -1)
            bmm = lambda a, w_ref: jnp.einsum('eij,ejk->eik', a, w_ref[...], preferred_element_type=F32)
            hg = bmm(x, wg_ref)
            y = bmm((hg * jax.nn.sigmoid(hg) * bmm(x, wu_ref)).astype(BF16), wdn_ref)
            for i in range(ne):
                @pl.when(k < nslots[i])
                def _store():
                    wcol = jnp.sum(jnp.where(hits[i], wd_ref[i], 0.0), axis=1, keepdims=True)
                    row0 = pl.multiple_of((bstart_ref[p + i] + k) * MOE_BLOCK, MOE_BLOCK)
                    yw_ref[pl.ds(row0, MOE_BLOCK), :] = (y[i] * wcol).astype(BF16)
            return c

        n_slots = nslots[0]
        for i in range(1, ne):
            n_slots = jnp.maximum(n_slots, nslots[i])
        lax.fori_loop(0, n_slots, slot, 0)

    @pl.when(j == n_expert_steps)
    def _combine():
        pt = rankt_ref[...]
        lane_e = lax.broadcasted_iota(jnp.int32, (st, N_EXPERTS), 1)
        li = lax.broadcasted_iota(jnp.int32, (st, MOE_BLOCK), 1)
        rows = MOE_CHUNK_BLOCKS * MOE_BLOCK

        def chunk(c, acc):
            hts = []
            for bi in range(MOE_CHUNK_BLOCKS):
                b = s * maxblk + c * MOE_CHUNK_BLOCKS + bi
                col = jnp.sum(jnp.where(lane_e == blke_ref[b], pt, 0.0), axis=1, keepdims=True)
                hts.append(jnp.where(col == (li + blkr_ref[b]).astype(F32), 1.0, 0.0).astype(BF16))
            ht = jnp.concatenate(hts, axis=1)
            return acc + jnp.dot(ht, yw_ref[pl.ds(pl.multiple_of(c * rows, rows), rows), :],
                                 preferred_element_type=F32)

        n_chunks = (nblk_ref[s] + MOE_CHUNK_BLOCKS - 1) // MOE_CHUNK_BLOCKS
        out_ref[...] = lax.fori_loop(0, n_chunks, chunk, jnp.zeros(out_ref.shape, F32))


def _moe2_routed(h2, wd, pos, rankt, cum, wg, wu, wdn):
    N, D = h2.shape
    E = N_EXPERTS
    n_tiles = N // TM
    tps = min(MOE_SUPER_TILES, n_tiles)
    ns = -(-n_tiles // tps)
    st = tps * TM
    pad = ns * st - N
    if pad:
        h2 = jnp.pad(h2, ((0, pad), (0, 0)))
        wd = jnp.pad(wd, ((0, 0), (0, pad)))
        pos = jnp.pad(pos, ((0, 0), (0, pad)), constant_values=-1)
        rankt = jnp.pad(rankt, ((0, pad), (0, 0)), constant_values=-1.0)
    maxblk = _moe_max_blocks(tps)
    meta = _moe2_meta(cum, n_tiles, tps)
    ne = MOE_EXPERTS_PER_STEP
    nq = E // ne
    ex = lambda j: jnp.minimum(j, nq - 1)
    grid_spec = pltpu.PrefetchScalarGridSpec(
        num_scalar_prefetch=5,
        grid=(ns, nq + 1),
        in_specs=[pl.BlockSpec((st, D), lambda s, j, *_: (s, 0)),
                  pl.BlockSpec((ne, 1, st), lambda s, j, *_: (ex(j), 0, s)),
                  pl.BlockSpec((st, E), lambda s, j, *_: (s, 0)),
                  pl.BlockSpec((ne, 1, st), lambda s, j, *_: (ex(j), 0, s)),
                  pl.BlockSpec((ne, D, D_EXPERT), lambda s, j, *_: (ex(j), 0, 0)),
                  pl.BlockSpec((ne, D, D_EXPERT), lambda s, j, *_: (ex(j), 0, 0)),
                  pl.BlockSpec((ne, D_EXPERT, D), lambda s, j, *_: (ex(j), 0, 0))],
        out_specs=pl.BlockSpec((st, D), lambda s, j, *_: (s, 0)),
        scratch_shapes=[pltpu.VMEM((maxblk * MOE_BLOCK, D), BF16)],
    )
    return pl.pallas_call(
        functools.partial(_moe2_kernel, tps=tps, maxblk=maxblk),
        grid_spec=grid_spec,
        out_shape=jax.ShapeDtypeStruct((ns * st, D), F32),
        compiler_params=_cparams(("arbitrary", "arbitrary"), MOE_VMEM_LIMIT),
        name="moe_routed",
    )(*meta, h2, pos.reshape(E, 1, ns * st), rankt, wd.reshape(E, 1, ns * st), wg, wu, wdn)


def _ffn_out_kernel(x1_ref, mod_ref, h2_ref, routed_ref, sg_ref, su_ref, sd_ref, post_ref, o_ref):
    h2 = h2_ref[...]
    hg = jnp.dot(h2, sg_ref[...], preferred_element_type=F32)
    hu = jnp.dot(h2, su_ref[...], preferred_element_type=F32)
    f = jnp.dot((hg * jax.nn.sigmoid(hg) * hu).astype(BF16), sd_ref[...], preferred_element_type=F32) + routed_ref[...]
    o_ref[0] = x1_ref[0] + mod_ref[0, 5:6, :] * _rms(f, post_ref[...])


def _ffn_out(x1, modl, h2, routed, sg, su, sd, post, n_lat):
    B, Tn, D = x1.shape
    nt = Tn // TM
    nb = modl.shape[0] - 1

    def mod_map(b, t):
        return (jnp.where(t >= n_lat, nb, b), 0, 0)

    row = pl.BlockSpec((1, TM, D), lambda b, t: (b, t, 0))
    flat = pl.BlockSpec((TM, D), lambda b, t: (b * nt + t, 0))
    return pl.pallas_call(
        _ffn_out_kernel,
        grid=(B, nt),
        in_specs=[row, pl.BlockSpec((1, 6, D), mod_map), flat, flat,
                  _const_spec(sg.shape), _const_spec(su.shape), _const_spec(sd.shape), _const_spec((1, D))],
        out_specs=row,
        out_shape=jax.ShapeDtypeStruct((B, Tn, D), F32),
        compiler_params=_cparams(("parallel", "parallel")),
        name="ffn_out",
    )(x1, modl, h2, routed, sg, su, sd, post)


def kernel(x, c, ctx, c_ctx, ada_w, ada_b, mix_norm_pre, mix_norm_post, ffn_norm_pre, ffn_norm_post, w_in, w_out, swa_sink, gdn_conv_w, gdn_a_log, gdn_dt_bias, gdn_norm_w, mla_q_norm, mla_w_uq, mla_kv_norm, mla_w_ukv, router_w, router_bias, expert_w_gate, expert_w_up, expert_w_down, shared_w_gate, shared_w_up, shared_w_down):
    B, S, D = x.shape
    L = ctx.shape[1]
    assert L == TM and S % (2 * TM) == 0
    T = S + L
    n_lat = S // TM
    xx = jnp.concatenate([x, ctx], axis=1)
    tabs = _rope_tables(S, T)
    cvecs = jnp.concatenate([c, c_ctx[None], jnp.zeros((8 - B - 1, D), F32)], axis=0)
    mods = _modulation(cvecs, ada_w, ada_b).reshape(DEPTH, 8, 6, D)[:, :B + 1]
    for layer in range(DEPTH):
        xx = _layer(layer, xx, mods[layer], tabs, S, layer == DEPTH - 1,
                    mix_norm_pre, mix_norm_post, ffn_norm_pre, ffn_norm_post, w_in, w_out, swa_sink, gdn_conv_w,
                    gdn_a_log, gdn_dt_bias, gdn_norm_w, mla_q_norm, mla_w_uq, mla_kv_norm, mla_w_ukv, router_w,
                    router_bias, expert_w_gate, expert_w_up, expert_w_down, shared_w_gate, shared_w_up, shared_w_down)
    return xx


def _layer(layer, xx, modl, tabs, S, last, mix_norm_pre, mix_norm_post, ffn_norm_pre, ffn_norm_post, w_in, w_out,
           swa_sink, gdn_conv_w, gdn_a_log, gdn_dt_bias, gdn_norm_w, mla_q_norm, mla_w_uq, mla_kv_norm, mla_w_ukv,
           router_w, router_bias, expert_w_gate, expert_w_up, expert_w_down, shared_w_gate, shared_w_up, shared_w_down):
    B, T, D = xx.shape
    n_lat = S // TM
    nt = n_lat if last else T // TM
    qa, ka, va, gx, z, ba, qm, km, vm = _project(
        xx, modl, mix_norm_pre[layer][None], _prep_w_in(w_in[layer]), _prep_w_uq(mla_w_uq[layer]),
        _prep_w_ukv(mla_w_ukv[layer]), jnp.pad(mla_q_norm[layer], (0, 256 - MLA_Q_RANK))[None],
        mla_kv_norm[layer][None], tabs, n_lat)
    oa = _swa_attention(swa_sink[layer], qa, ka, va, S, nt)
    oc = _mla_attention(qm, km, vm, S, not last)
    gp = jnp.zeros((2, LANE), F32)
    g0 = _BA_LANE + 2 * GDN_HEADS
    gp = gp.at[0, g0:g0 + 2 * GDN_HEADS].set(gdn_a_log[layer].reshape(-1))
    gp = gp.at[1, g0:g0 + 2 * GDN_HEADS].set(gdn_dt_bias[layer].reshape(-1))
    p1, p2, p3, gt = _gdn_local(gx, gdn_conv_w[layer], ba, gp, n_lat)
    o_f, o_b = _gdn_scan(p1, p2, p3, gt, S)

    x1, h2, wd, pos, rankt, cum = _mix_out(
        xx, modl, oa, o_f, o_b, z, oc, gdn_norm_w[layer][None], _prep_w_out(w_out[layer]),
        mix_norm_post[layer][None], ffn_norm_pre[layer][None], router_w[layer].T.astype(BF16),
        router_bias[layer][:, None], n_lat, nt, min(MOE_SUPER_TILES, B * nt))
    routed = _moe2_routed(h2, wd, pos, rankt, cum, expert_w_gate[layer].astype(BF16),
                          expert_w_up[layer].astype(BF16), expert_w_down[layer].astype(BF16))
    return _ffn_out(x1, modl, h2, routed, shared_w_gate[layer].astype(BF16), shared_w_up[layer].astype(BF16),
                    shared_w_down[layer].astype(BF16), ffn_norm_post[layer][None], n_lat)
```
